```python
import math
import jax, jax.numpy as jnp
from jax import lax
import numpy as np

D_MODEL = 1024
BATCH = 8
SEQ = 4096
DEPTH = 1

N_META = 16
BLOCK_Q = 128
HEAD_DIM = 64
N_HEADS_DIFF = 8
DIFF_V_DIM = 2 * HEAD_DIM
D_QK_DIFF = 2 * N_HEADS_DIFF * HEAD_DIM
D_DIFF = N_HEADS_DIFF * DIFF_V_DIM
N_HEADS_SB = 16
D_SB = N_HEADS_SB * HEAD_DIM
D_FF = 4 * D_MODEL
ROPE_THETA = 10000.0
EPS = 1e-6
IN_SIZES = [D_QK_DIFF, D_QK_DIFF, D_DIFF, D_SB, D_SB, D_SB, D_MODEL, D_MODEL]
D_IN = sum(IN_SIZES)

kernel_name = "hybrid_diffattn_stickbreaking_gated"


def lambda_init_fn(layer_idx):
    return 0.8 - 0.6 * math.exp(-0.3 * layer_idx)


def rmsnorm(x, g):
    xf = x.astype(jnp.float32)
    y = xf * lax.rsqrt(jnp.mean(xf * xf, axis=-1, keepdims=True) + EPS)
    return (y * g.astype(jnp.float32)).astype(x.dtype)


def rope(x, pos):
    d = x.shape[-1]
    inv = ROPE_THETA ** (-jnp.arange(0, d, 2, dtype=jnp.float32) / d)
    ang = pos.astype(jnp.float32)[:, None] * inv[None, :]
    cos, sin = jnp.cos(ang), jnp.sin(ang)
    xf = x.astype(jnp.float32)
    x1, x2 = xf[..., : d // 2], xf[..., d // 2:]
    out = jnp.concatenate([x1 * cos - x2 * sin, x2 * cos + x1 * sin], axis=-1)
    return out.astype(x.dtype)


def block_bounds():
    bounds = [(0, N_META)]
    for s in range(0, SEQ, BLOCK_Q):
        bounds.append((N_META + s, N_META + min(SEQ, s + BLOCK_Q)))
    return bounds


def diff_attn_block(qa, ka, va, lam, start, end):
    B = qa.shape[0]
    tq = end - start
    s = jnp.einsum('bhqd,bhkd->bhqk', qa[:, :, start:end], ka[:, :, :end]).astype(jnp.float32)
    s = s / math.sqrt(HEAD_DIM)
    causal = jnp.arange(start, end)[:, None] >= jnp.arange(end)[None, :]
    p = jax.nn.softmax(jnp.where(causal, s, -jnp.inf), axis=-1)
    p = p.reshape(B, N_HEADS_DIFF, 2, tq, end)
    a = p[:, :, 0] - lam * p[:, :, 1]
    return jnp.einsum('bhqk,bhkd->bhqd', a.astype(va.dtype), va[:, :, :end])


def stick_breaking_block(qb, kb, vb, start, end):
    z = jnp.einsum('bhqd,bhkd->bhqk', qb[:, :, start:end], kb[:, :, :end]).astype(jnp.float32)
    z = z / math.sqrt(HEAD_DIM)
    strict = jnp.arange(end)[None, :] < jnp.arange(start, end)[:, None]
    log_1m = jnp.where(strict, jax.nn.log_sigmoid(-z), 0.0)
    rcum = lax.cumsum(log_1m, axis=3, reverse=True)
    excl = jnp.concatenate([rcum[..., 1:], jnp.zeros_like(rcum[..., :1])], axis=-1)
    log_a = jax.nn.log_sigmoid(z) + excl
    a = jnp.where(strict, jnp.exp(log_a), 0.0)
    return jnp.einsum('bhqk,bhkd->bhqd', a.astype(vb.dtype), vb[:, :, :end])


def setup_inputs(seed: int = 0) -> dict:
    key = jax.random.key(seed)
    ks = jax.random.split(key, 16)
    nrm = jax.random.normal
    f32 = jnp.float32
    return {
        "x": nrm(ks[0], (BATCH, SEQ, D_MODEL), f32),
        "meta_tokens": nrm(ks[1], (N_META, D_MODEL), f32),
        "norm_mix_g": 1.0 + 0.02 * nrm(ks[2], (DEPTH, D_MODEL), f32),
        "w_in": nrm(ks[3], (DEPTH, D_MODEL, D_IN), f32) * D_MODEL ** -0.5,
        "lambda_q1": 0.1 * nrm(ks[4], (DEPTH, HEAD_DIM), f32),
        "lambda_k1": 0.1 * nrm(ks[5], (DEPTH, HEAD_DIM), f32),
        "lambda_q2": 0.1 * nrm(ks[6], (DEPTH, HEAD_DIM), f32),
        "lambda_k2": 0.1 * nrm(ks[7], (DEPTH, HEAD_DIM), f32),
        "subln_g": 1.0 + 0.02 * nrm(ks[8], (DEPTH, DIFF_V_DIM), f32),
        "w_out_a": nrm(ks[9], (DEPTH, D_DIFF, D_MODEL), f32) * D_DIFF ** -0.5,
        "w_out_b": nrm(ks[10], (DEPTH, D_SB, D_MODEL), f32) * D_SB ** -0.5,
        "w_out": nrm(ks[11], (DEPTH, D_MODEL, D_MODEL), f32) * D_MODEL ** -0.5,
        "norm_mlp_g": 1.0 + 0.02 * nrm(ks[12], (DEPTH, D_MODEL), f32),
        "w_up": nrm(ks[13], (DEPTH, D_MODEL, D_FF), f32) * D_MODEL ** -0.5,
        "w_down": nrm(ks[14], (DEPTH, D_FF, D_MODEL), f32) * D_FF ** -0.5,
        "norm_final_g": 1.0 + 0.02 * nrm(ks[15], (D_MODEL,), f32),
    }


def reference(x, meta_tokens, norm_mix_g, w_in, lambda_q1, lambda_k1, lambda_q2, lambda_k2,
              subln_g, w_out_a, w_out_b, w_out, norm_mlp_g, w_up, w_down, norm_final_g):
    B = x.shape[0]
    meta = jnp.broadcast_to(meta_tokens.astype(x.dtype)[None], (B, N_META, D_MODEL))
    h = jnp.concatenate([meta, x], axis=1)
    L = h.shape[1]
    pos = jnp.arange(L, dtype=jnp.int32)
    split_idx = list(np.cumsum(IN_SIZES)[:-1])
    bounds = block_bounds()

    for layer in range(DEPTH):
        lam_init = lambda_init_fn(layer)
        hn = rmsnorm(h, norm_mix_g[layer])
        proj = hn @ w_in[layer]
        qa, ka, va, qb, kb, vb, ga, gb = jnp.split(proj, split_idx, axis=-1)
        qa = rope(qa.reshape(B, L, 2 * N_HEADS_DIFF, HEAD_DIM).transpose(0, 2, 1, 3), pos)
        ka = rope(ka.reshape(B, L, 2 * N_HEADS_DIFF, HEAD_DIM).transpose(0, 2, 1, 3), pos)
        va = va.reshape(B, L, N_HEADS_DIFF, DIFF_V_DIM).transpose(0, 2, 1, 3)
        qb = qb.reshape(B, L, N_HEADS_SB, HEAD_DIM).transpose(0, 2, 1, 3)
        kb = kb.reshape(B, L, N_HEADS_SB, HEAD_DIM).transpose(0, 2, 1, 3)
        vb = vb.reshape(B, L, N_HEADS_SB, HEAD_DIM).transpose(0, 2, 1, 3)
        lam = (jnp.exp(jnp.sum(lambda_q1[layer].astype(jnp.float32) * lambda_k1[layer].astype(jnp.float32)))
               - jnp.exp(jnp.sum(lambda_q2[layer].astype(jnp.float32) * lambda_k2[layer].astype(jnp.float32)))
               + lam_init)

        outs_a, outs_b = [], []
        for (s0, e0) in bounds:
            outs_a.append(diff_attn_block(qa, ka, va, lam, s0, e0))
            outs_b.append(stick_breaking_block(qb, kb, vb, s0, e0))
        oa = jnp.concatenate(outs_a, axis=2)
        ob = jnp.concatenate(outs_b, axis=2)
        oa = rmsnorm(oa, subln_g[layer]) * (1.0 - lam_init)
        oa = oa.transpose(0, 2, 1, 3).reshape(B, L, D_DIFF)
        ob = ob.transpose(0, 2, 1, 3).reshape(B, L, D_SB)
        ua = oa @ w_out_a[layer]
        ub = ob @ w_out_b[layer]
        merged = jax.nn.sigmoid(ga) * ua + jax.nn.sigmoid(gb) * ub
        h = h + merged @ w_out[layer]
        hm = rmsnorm(h, norm_mlp_g[layer])
        h = h + jnp.square(jax.nn.relu(hm @ w_up[layer])) @ w_down[layer]

    h = rmsnorm(h, norm_final_g)
    return h[:, N_META:]
```

```python
import functools
import math

import jax
import jax.numpy as jnp
from jax import lax
from jax.experimental import pallas as pl
from jax.experimental.pallas import tpu as pltpu

D_MODEL = 1024
N_META = 16
HEAD_DIM = 64
N_PAIR = 8
PAIR = 2 * HEAD_DIM
D_FF = 4 * D_MODEL
N_SEG = 8
ROPE_THETA = 10000.0
EPS = 1e-6
LAM_INIT = 0.8 - 0.6 * math.exp(-0.3 * 0)

LANES = 128
META_PAD = 128
TQ = 256
TK = 256
NEG = -1e30
Q_SCALE = math.log2(math.e) / math.sqrt(HEAD_DIM)
VMEM_LIMIT = 48 * 1024 * 1024

F32 = jnp.float32
BF16 = jnp.bfloat16


def _nt_dot(a, b):
    return lax.dot_general(a, b, (((1,), (1,)), ((), ())), preferred_element_type=F32)


def _dot(a, b):
    return jnp.dot(a, b, preferred_element_type=F32)


def _inproj_kernel(x_ref, g_ref, w_ref, cos_ref, sin_ref,
                   qa_ref, ka_ref, vat_ref, qb_ref, kb_ref, vbt_ref, ga_ref, gb_ref,
                   hn_ref, y_ref, *, tk):
    j = pl.program_id(1)
    tm = x_ref.shape[0]

    @pl.when(j == 0)
    def _():
        x = x_ref[...]
        r = lax.rsqrt(jnp.mean(x * x, axis=-1, keepdims=True) + EPS)
        hn_ref[...] = (x * r * g_ref[...]).astype(BF16)

    y_ref[...] = _dot(hn_ref[...], w_ref[...])

    def rope_store(o_ref, scale):
        lane = lax.broadcasted_iota(jnp.int32, (tm, LANES), 1)
        first_half = (lane % HEAD_DIM) < (HEAD_DIM // 2)
        cos = cos_ref[...]
        sin = sin_ref[...]
        for c in range(D_MODEL // LANES):
            yc = y_ref[:, c * LANES:(c + 1) * LANES]
            partner = jnp.where(first_half,
                                pltpu.roll(yc, LANES - HEAD_DIM // 2, 1),
                                pltpu.roll(yc, HEAD_DIM // 2, 1))
            out = yc * cos + partner * sin
            if scale != 1.0:
                out = out * scale
            o_ref[:, c * LANES:(c + 1) * LANES] = out.astype(o_ref.dtype)

    def transposed_store(o_ref):
        for h in range(N_PAIR):
            t = y_ref[:, h * LANES:(h + 1) * LANES].T.astype(o_ref.dtype)
            for c in range(tm // tk):
                o_ref[0, h, c] = t[:, c * tk:(c + 1) * tk]

    @pl.when(j == 0)
    def _():
        rope_store(qa_ref, Q_SCALE)

    @pl.when(j == 1)
    def _():
        rope_store(ka_ref, 1.0)

    @pl.when(j == 2)
    def _():
        transposed_store(vat_ref)

    @pl.when(j == 3)
    def _():
        qb_ref[...] = (y_ref[...] * Q_SCALE).astype(qb_ref.dtype)

    @pl.when(j == 4)
    def _():
        kb_ref[...] = y_ref[...].astype(kb_ref.dtype)

    @pl.when(j == 5)
    def _():
        transposed_store(vbt_ref)

    @pl.when(j == 6)
    def _():
        ga_ref[...] = y_ref[...]

    @pl.when(j == 7)
    def _():
        gb_ref[...] = y_ref[...]


def _inproj(x2d, g, w_bf16, cos, sin, *, rows_per_batch, tm, tk):
    m = x2d.shape[0]
    nb = m // rows_per_batch
    tiles_per_batch = rows_per_batch // tm
    nblk = rows_per_batch // tk
    row_block = pl.BlockSpec((tm, D_MODEL), lambda i, j: (i, 0))
    vt_block = pl.BlockSpec((1, N_PAIR, tm // tk, LANES, tk),
                            lambda i, j: (i // tiles_per_batch, 0, i % tiles_per_batch, 0, 0))
    nat = jax.ShapeDtypeStruct((m, D_MODEL), BF16)
    vt = jax.ShapeDtypeStruct((nb, N_PAIR, nblk, LANES, tk), BF16)
    gate = jax.ShapeDtypeStruct((m, D_MODEL), F32)
    return pl.pallas_call(
        functools.partial(_inproj_kernel, tk=tk),
        grid=(m // tm, N_SEG),
        in_specs=[
            row_block,
            pl.BlockSpec((1, D_MODEL), lambda i, j: (0, 0)),
            pl.BlockSpec((D_MODEL, D_MODEL), lambda i, j: (0, j)),
            pl.BlockSpec((tm, LANES), lambda i, j: (i, 0)),
            pl.BlockSpec((tm, LANES), lambda i, j: (i, 0)),
        ],
        out_specs=[row_block, row_block, vt_block, row_block, row_block, vt_block, row_block, row_block],
        out_shape=[nat, nat, vt, nat, nat, vt, gate, gate],
        scratch_shapes=[pltpu.VMEM((tm, D_MODEL), BF16), pltpu.VMEM((tm, D_MODEL), F32)],
        compiler_params=pltpu.CompilerParams(
            dimension_semantics=("arbitrary", "arbitrary"), vmem_limit_bytes=VMEM_LIMIT),
        name="inproj",
    )(x2d, g, w_bf16, cos, sin)


def _rope_tables(pos):
    half = HEAD_DIM // 2
    inv = ROPE_THETA ** (-jnp.arange(0, HEAD_DIM, 2, dtype=F32) / HEAD_DIM)
    ang = pos.astype(F32)[:, None] * inv[None, :]
    cos, sin = jnp.cos(ang), jnp.sin(ang)
    reps = LANES // HEAD_DIM
    cos_t = jnp.tile(jnp.concatenate([cos, cos], axis=-1), (1, reps))
    sin_t = jnp.tile(jnp.concatenate([-sin, sin], axis=-1), (1, reps))
    return cos_t, sin_t


def _split_pair(q):
    lane = lax.broadcasted_iota(jnp.int32, q.shape, 1)
    zero = jnp.zeros_like(q)
    return jnp.concatenate([jnp.where(lane < HEAD_DIM, q, zero),
                            jnp.where(lane >= HEAD_DIM, q, zero)], axis=0)


def _diff_kernel(lq1_ref, lk1_ref, lq2_ref, lk2_ref, g_ref,
                 q_ref, k_ref, vt_ref, km_ref, vtm_ref, o_ref, acc_ref):
    i = pl.program_id(2)
    tq = q_ref.shape[1]
    tk = vt_ref.shape[4]
    qq = _split_pair(q_ref[0])

    acc_ref[...] = jnp.zeros_like(acc_ref)

    def update(s, vt, m, l):
        m_new = jnp.maximum(m, jnp.max(s, axis=0, keepdims=True))
        alpha = jnp.exp2(m - m_new)
        p = jnp.exp2(s - m_new)
        l_new = alpha * l + jnp.sum(p, axis=0, keepdims=True)
        acc_ref[...] = alpha * acc_ref[...] + _dot(vt, p.astype(BF16))
        return m_new, l_new

    m0 = jnp.full((1, 2 * tq), NEG, F32)
    l0 = jnp.zeros((1, 2 * tq), F32)

    row = lax.broadcasted_iota(jnp.int32, (tk, 2 * tq), 0)
    col = lax.broadcasted_iota(jnp.int32, (tk, 2 * tq), 1) % tq
    k_diag = k_ref[0, pl.ds(pl.multiple_of(i * tk, tk), tk), :]
    s = jnp.where(row <= col, _nt_dot(k_diag, qq), NEG)
    m, l = update(s, vt_ref[0, 0, i], m0, l0)

    def body(j, carry):
        kj = k_ref[0, pl.ds(pl.multiple_of(j * tk, tk), tk), :]
        return update(_nt_dot(kj, qq), vt_ref[0, 0, j], *carry)

    m, l = lax.fori_loop(0, i, body, (m, l))

    rowm = lax.broadcasted_iota(jnp.int32, (META_PAD, 2 * tq), 0)
    s = jnp.where(rowm < N_META, _nt_dot(km_ref[...], qq), NEG)
    m, l = update(s, vtm_ref[0], m, l)

    lam = (jnp.exp(jnp.sum(lq1_ref[...] * lk1_ref[...], axis=1, keepdims=True))
           - jnp.exp(jnp.sum(lq2_ref[...] * lk2_ref[...], axis=1, keepdims=True)) + LAM_INIT)
    inv_l = 1.0 / l
    acc = acc_ref[...]
    o = acc[:, :tq] * inv_l[:, :tq] - acc[:, tq:] * (lam * inv_l[:, tq:])
    o = o * lax.rsqrt(jnp.mean(o * o, axis=0, keepdims=True) + EPS)
    o_ref[0] = (o.T * g_ref[...] * (1.0 - LAM_INIT)).astype(o_ref.dtype)


def _diff_attn(lam_params, g, q, k, vt, km, vtm):
    b, seq, _ = q.shape
    nq = seq // TQ
    small = pl.BlockSpec((1, HEAD_DIM), lambda bi, h, i: (0, 0))
    return pl.pallas_call(
        _diff_kernel,
        grid=(b, N_PAIR, nq),
        in_specs=[
            small, small, small, small,
            pl.BlockSpec((1, PAIR), lambda bi, h, i: (0, 0)),
            pl.BlockSpec((1, TQ, PAIR), lambda bi, h, i: (bi, i, h)),
            pl.BlockSpec((1, seq, PAIR), lambda bi, h, i: (bi, 0, h)),
            pl.BlockSpec((1, 1, seq // TK, PAIR, TK), lambda bi, h, i: (bi, h, 0, 0, 0)),
            pl.BlockSpec((META_PAD, PAIR), lambda bi, h, i: (0, h)),
            pl.BlockSpec((1, PAIR, META_PAD), lambda bi, h, i: (h, 0, 0)),
        ],
        out_specs=pl.BlockSpec((1, TQ, PAIR), lambda bi, h, i: (bi, i, h)),
        out_shape=jax.ShapeDtypeStruct((b, seq, D_MODEL), BF16),
        scratch_shapes=[pltpu.VMEM((PAIR, 2 * TQ), F32)],
        compiler_params=pltpu.CompilerParams(
            dimension_semantics=("arbitrary", "arbitrary", "arbitrary"), vmem_limit_bytes=VMEM_LIMIT),
        name="diff_attn",
    )(*lam_params, g, q, k, vt, km, vtm)


def _sb_kernel(q_ref, k_ref, vt_ref, km_ref, vtm_ref, o_ref, acc_ref):
    i = pl.program_id(2)
    tq = q_ref.shape[1]
    tk = vt_ref.shape[4]
    qq = _split_pair(q_ref[0])

    ur = lax.broadcasted_iota(jnp.int32, (tk, tk), 0)
    uc = lax.broadcasted_iota(jnp.int32, (tk, tk), 1)
    u = jnp.where(uc >= ur, 1.0, 0.0).astype(BF16)

    acc_ref[...] = jnp.zeros_like(acc_ref)

    def update(z, vt, u_blk, carry, visible):
        sp = jnp.maximum(z, 0.0) + jnp.log2(1.0 + jnp.exp2(-jnp.abs(z)))
        if visible is not None:
            sp = jnp.where(visible, sp, 0.0)
        hi = sp.astype(BF16)
        lo = (sp - hi.astype(F32)).astype(BF16)
        tot = _dot(u_blk, hi) + _dot(u_blk, lo) + carry
        a = jnp.exp2(z - tot)
        if visible is not None:
            a = jnp.where(visible, a, 0.0)
        a = a.astype(BF16)
        acc_ref[0:HEAD_DIM, :] += _dot(vt[0:HEAD_DIM, :], a[:, :tq])
        acc_ref[HEAD_DIM:PAIR, :] += _dot(vt[HEAD_DIM:PAIR, :], a[:, tq:])
        return tot[0:1, :]

    carry = jnp.zeros((1, 2 * tq), F32)

    row = lax.broadcasted_iota(jnp.int32, (tk, 2 * tq), 0)
    col = lax.broadcasted_iota(jnp.int32, (tk, 2 * tq), 1) % tq
    k_diag = k_ref[0, pl.ds(pl.multiple_of(i * tk, tk), tk), :]
    carry = update(_nt_dot(k_diag, qq), vt_ref[0, 0, i], u, carry, row < col)

    def body(t, c):
        j = i - 1 - t
        kj = k_ref[0, pl.ds(pl.multiple_of(j * tk, tk), tk), :]
        return update(_nt_dot(kj, qq), vt_ref[0, 0, j], u, c, None)

    carry = lax.fori_loop(0, i, body, carry)

    rowm = lax.broadcasted_iota(jnp.int32, (META_PAD, 2 * tq), 0)
    update(_nt_dot(km_ref[...], qq), vtm_ref[0], u[:META_PAD, :META_PAD], carry, rowm < N_META)

    o_ref[0] = acc_ref[...].T.astype(o_ref.dtype)


def _sb_attn(q, k, vt, km, vtm):
    b, seq, _ = q.shape
    nq = seq // TQ
    return pl.pallas_call(
        _sb_kernel,
        grid=(b, N_PAIR, nq),
        in_specs=[
            pl.BlockSpec((1, TQ, PAIR), lambda bi, h, i: (bi, i, h)),
            pl.BlockSpec((1, seq, PAIR), lambda bi, h, i: (bi, 0, h)),
            pl.BlockSpec((1, 1, seq // TK, PAIR, TK), lambda bi, h, i: (bi, h, 0, 0, 0)),
            pl.BlockSpec((META_PAD, PAIR), lambda bi, h, i: (0, h)),
            pl.BlockSpec((1, PAIR, META_PAD), lambda bi, h, i: (h, 0, 0)),
        ],
        out_specs=pl.BlockSpec((1, TQ, PAIR), lambda bi, h, i: (bi, i, h)),
        out_shape=jax.ShapeDtypeStruct((b, seq, D_MODEL), BF16),
        scratch_shapes=[pltpu.VMEM((PAIR, TQ), F32)],
        compiler_params=pltpu.CompilerParams(
            dimension_semantics=("arbitrary", "arbitrary", "arbitrary"), vmem_limit_bytes=VMEM_LIMIT),
        name="sb_attn",
    )(q, k, vt, km, vtm)


def _mix_kernel(x_ref, oa_ref, ob_ref, ga_ref, gb_ref, wa_ref, wb_ref, wo_ref, g_ref, h_ref, hm_ref):
    ua = _dot(oa_ref[...], wa_ref[...])
    ub = _dot(ob_ref[...], wb_ref[...])
    merged = jax.nn.sigmoid(ga_ref[...]) * ua + jax.nn.sigmoid(gb_ref[...]) * ub
    h = x_ref[...] + _dot(merged.astype(BF16), wo_ref[...])
    h_ref[...] = h
    r = lax.rsqrt(jnp.mean(h * h, axis=-1, keepdims=True) + EPS)
    hm_ref[...] = (h * r * g_ref[...]).astype(BF16)


def _mix_out(x2d, oa, ob, ga, gb, wa, wb, wo, g, *, tm):
    m = x2d.shape[0]
    row = pl.BlockSpec((tm, D_MODEL), lambda i: (i, 0))
    wspec = pl.BlockSpec((D_MODEL, D_MODEL), lambda i: (0, 0))
    return pl.pallas_call(
        _mix_kernel,
        grid=(m // tm,),
        in_specs=[row, row, row, row, row, wspec, wspec, wspec, pl.BlockSpec((1, D_MODEL), lambda i: (0, 0))],
        out_specs=[row, row],
        out_shape=[jax.ShapeDtypeStruct((m, D_MODEL), F32), jax.ShapeDtypeStruct((m, D_MODEL), BF16)],
        compiler_params=pltpu.CompilerParams(
            dimension_semantics=("arbitrary",), vmem_limit_bytes=VMEM_LIMIT),
        name="mix_out",
    )(x2d, oa, ob, ga, gb, wa, wb, wo, g)


def _mlp_kernel(hm_ref, h_ref, wu_ref, wd_ref, g_ref, o_ref, acc_ref):
    f = pl.program_id(1)

    @pl.when(f == 0)
    def _():
        acc_ref[...] = jnp.zeros_like(acc_ref)

    up = jnp.maximum(_dot(hm_ref[...], wu_ref[...]), 0.0)
    acc_ref[...] += _dot((up * up).astype(BF16), wd_ref[...])

    @pl.when(f == pl.num_programs(1) - 1)
    def _():
        h = h_ref[...] + acc_ref[...]
        r = lax.rsqrt(jnp.mean(h * h, axis=-1, keepdims=True) + EPS)
        o_ref[...] = h * r * g_ref[...]


def _mlp(hm, h, wu, wd, g, *, tm, tf):
    m = h.shape[0]
    row = pl.BlockSpec((tm, D_MODEL), lambda i, f: (i, 0))
    return pl.pallas_call(
        _mlp_kernel,
        grid=(m // tm, D_FF // tf),
        in_specs=[row, row,
                  pl.BlockSpec((D_MODEL, tf), lambda i, f: (0, f)),
                  pl.BlockSpec((tf, D_MODEL), lambda i, f: (f, 0)),
                  pl.BlockSpec((1, D_MODEL), lambda i, f: (0, 0))],
        out_specs=row,
        out_shape=jax.ShapeDtypeStruct((m, D_MODEL), F32),
        scratch_shapes=[pltpu.VMEM((tm, D_MODEL), F32)],
        compiler_params=pltpu.CompilerParams(
            dimension_semantics=("arbitrary", "arbitrary"), vmem_limit_bytes=VMEM_LIMIT),
        name="mlp",
    )(hm, h, wu, wd, g)


def kernel(x, meta_tokens, norm_mix_g, w_in, lambda_q1, lambda_k1, lambda_q2, lambda_k2, subln_g,
           w_out_a, w_out_b, w_out, norm_mlp_g, w_up, w_down, norm_final_g):
    b, seq, d = x.shape
    assert d == D_MODEL and seq % TQ == 0 and TQ == TK
    assert w_in.shape == (1, D_MODEL, N_SEG * D_MODEL)
    x2d = x.reshape(b * seq, d)
    g_mix = norm_mix_g[0].reshape(1, d)
    w_in_b = w_in[0].astype(BF16)

    cos_f, sin_f = _rope_tables(jnp.arange(N_META, N_META + seq, dtype=jnp.int32))
    cos_f = jnp.tile(cos_f, (b, 1))
    sin_f = jnp.tile(sin_f, (b, 1))
    qa, ka, vat, qb, kb, vbt, ga, gb = _inproj(
        x2d, g_mix, w_in_b, cos_f, sin_f, rows_per_batch=seq, tm=512, tk=TK)

    meta_pad = jnp.zeros((META_PAD, d), x.dtype).at[:N_META].set(meta_tokens.astype(x.dtype))
    cos_m, sin_m = _rope_tables(jnp.arange(META_PAD, dtype=jnp.int32))
    _, kam, vatm, _, kbm, vbtm, _, _ = _inproj(
        meta_pad, g_mix, w_in_b, cos_m, sin_m, rows_per_batch=META_PAD, tm=META_PAD, tk=META_PAD)
    vatm = vatm.reshape(N_PAIR, PAIR, META_PAD)
    vbtm = vbtm.reshape(N_PAIR, PAIR, META_PAD)

    lam_params = [p[0].reshape(1, HEAD_DIM) for p in (lambda_q1, lambda_k1, lambda_q2, lambda_k2)]
    oa = _diff_attn(lam_params, subln_g[0].reshape(1, PAIR),
                    qa.reshape(b, seq, d), ka.reshape(b, seq, d), vat, kam, vatm)
    ob = _sb_attn(qb.reshape(b, seq, d), kb.reshape(b, seq, d), vbt, kbm, vbtm)

    h1, hm = _mix_out(x2d, oa.reshape(b * seq, d), ob.reshape(b * seq, d), ga, gb,
                      w_out_a[0].astype(BF16), w_out_b[0].astype(BF16), w_out[0].astype(BF16),
                      norm_mlp_g[0].reshape(1, d), tm=512)
    out = _mlp(hm, h1, w_up[0].astype(BF16), w_down[0].astype(BF16),
               norm_final_g.reshape(1, d), tm=1024, tf=512)
    return out.reshape(b, seq, d)
```

```python
import functools
import math

import jax
import jax.numpy as jnp
from jax import lax
from jax.experimental import pallas as pl
from jax.experimental.pallas import tpu as pltpu

D_MODEL = 1024
N_META = 16
HEAD_DIM = 64
N_PAIR = 8
PAIR = 2 * HEAD_DIM
D_FF = 4 * D_MODEL
N_SEG = 8
ROPE_THETA = 10000.0
EPS = 1e-6
LAM_INIT = 0.8 - 0.6 * math.exp(-0.3 * 0)

LANES = 128
META_PAD = 128
TQ = 256
TK = 256
NG = 4
NEG = -1e30
SB_DONE_BITS = 160.0
Q_SCALE = math.log2(math.e) / math.sqrt(HEAD_DIM)
VMEM_LIMIT = 48 * 1024 * 1024

F32 = jnp.float32
BF16 = jnp.bfloat16


def _nt_dot(a, b):
    return lax.dot_general(a, b, (((1,), (1,)), ((), ())), preferred_element_type=F32)


def _dot(a, b):
    return jnp.dot(a, b, preferred_element_type=F32)


def _inproj_kernel(x_ref, g_ref, w_ref, cos_ref, sin_ref,
                   qa_ref, ka_ref, vat_ref, qb_ref, kb_ref, vbt_ref, ga_ref, gb_ref,
                   hn_ref, y_ref, *, tk):
    j = pl.program_id(1)
    tm = x_ref.shape[0]

    @pl.when(j == 0)
    def _():
        x = x_ref[...]
        r = lax.rsqrt(jnp.mean(x * x, axis=-1, keepdims=True) + EPS)
        hn_ref[...] = (x * r * g_ref[...]).astype(BF16)

    y_ref[...] = _dot(hn_ref[...], w_ref[...])

    def rope_store(o_ref, scale):
        lane = lax.broadcasted_iota(jnp.int32, (tm, LANES), 1)
        first_half = (lane % HEAD_DIM) < (HEAD_DIM // 2)
        cos = cos_ref[...]
        sin = sin_ref[...]
        for c in range(D_MODEL // LANES):
            yc = y_ref[:, c * LANES:(c + 1) * LANES]
            partner = jnp.where(first_half,
                                pltpu.roll(yc, LANES - HEAD_DIM // 2, 1),
                                pltpu.roll(yc, HEAD_DIM // 2, 1))
            out = yc * cos + partner * sin
            if scale != 1.0:
                out = out * scale
            o_ref[:, c * LANES:(c + 1) * LANES] = out.astype(o_ref.dtype)

    def transposed_store(o_ref):
        for h in range(N_PAIR):
            t = y_ref[:, h * LANES:(h + 1) * LANES].T.astype(o_ref.dtype)
            for c in range(tm // tk):
                o_ref[0, h, c] = t[:, c * tk:(c + 1) * tk]

    @pl.when(j == 0)
    def _():
        rope_store(qa_ref, Q_SCALE)

    @pl.when(j == 1)
    def _():
        rope_store(ka_ref, 1.0)

    @pl.when(j == 2)
    def _():
        transposed_store(vat_ref)

    @pl.when(j == 3)
    def _():
        qb_ref[...] = (y_ref[...] * Q_SCALE).astype(qb_ref.dtype)

    @pl.when(j == 4)
    def _():
        kb_ref[...] = y_ref[...].astype(kb_ref.dtype)

    @pl.when(j == 5)
    def _():
        transposed_store(vbt_ref)

    @pl.when(j == 6)
    def _():
        ga_ref[...] = y_ref[...]

    @pl.when(j == 7)
    def _():
        gb_ref[...] = y_ref[...]


def _inproj(x2d, g, w_bf16, cos, sin, *, rows_per_batch, tm, tk):
    m = x2d.shape[0]
    nb = m // rows_per_batch
    tiles_per_batch = rows_per_batch // tm
    nblk = rows_per_batch // tk
    row_block = pl.BlockSpec((tm, D_MODEL), lambda i, j: (i, 0))
    vt_block = pl.BlockSpec((1, N_PAIR, tm // tk, LANES, tk),
                            lambda i, j: (i // tiles_per_batch, 0, i % tiles_per_batch, 0, 0))
    nat = jax.ShapeDtypeStruct((m, D_MODEL), BF16)
    vt = jax.ShapeDtypeStruct((nb, N_PAIR, nblk, LANES, tk), BF16)
    gate = jax.ShapeDtypeStruct((m, D_MODEL), F32)
    return pl.pallas_call(
        functools.partial(_inproj_kernel, tk=tk),
        grid=(m // tm, N_SEG),
        in_specs=[
            row_block,
            pl.BlockSpec((1, D_MODEL), lambda i, j: (0, 0)),
            pl.BlockSpec((D_MODEL, D_MODEL), lambda i, j: (0, j)),
            pl.BlockSpec((tm, LANES), lambda i, j: (i, 0)),
            pl.BlockSpec((tm, LANES), lambda i, j: (i, 0)),
        ],
        out_specs=[row_block, row_block, vt_block, row_block, row_block, vt_block, row_block, row_block],
        out_shape=[nat, nat, vt, nat, nat, vt, gate, gate],
        scratch_shapes=[pltpu.VMEM((tm, D_MODEL), BF16), pltpu.VMEM((tm, D_MODEL), F32)],
        compiler_params=pltpu.CompilerParams(
            dimension_semantics=("arbitrary", "arbitrary"), vmem_limit_bytes=VMEM_LIMIT),
        name="inproj",
    )(x2d, g, w_bf16, cos, sin)


def _rope_tables(pos):
    half = HEAD_DIM // 2
    inv = ROPE_THETA ** (-jnp.arange(0, HEAD_DIM, 2, dtype=F32) / HEAD_DIM)
    ang = pos.astype(F32)[:, None] * inv[None, :]
    cos, sin = jnp.cos(ang), jnp.sin(ang)
    reps = LANES // HEAD_DIM
    cos_t = jnp.tile(jnp.concatenate([cos, cos], axis=-1), (1, reps))
    sin_t = jnp.tile(jnp.concatenate([-sin, sin], axis=-1), (1, reps))
    return cos_t, sin_t


def _split_pair(q):
    lane = lax.broadcasted_iota(jnp.int32, q.shape, 1)
    zero = jnp.zeros_like(q)
    return jnp.concatenate([jnp.where(lane < HEAD_DIM, q, zero),
                            jnp.where(lane >= HEAD_DIM, q, zero)], axis=0)


def _lanes(h):
    return slice(h * PAIR, (h + 1) * PAIR)


def _attn_specs(b, seq):
    w = NG * PAIR
    in_specs = [
        pl.BlockSpec((1, TQ, w), lambda bi, hg, i: (bi, i, hg)),
        pl.BlockSpec((1, seq, w), lambda bi, hg, i: (bi, 0, hg)),
        pl.BlockSpec((1, NG, seq // TK, PAIR, TK), lambda bi, hg, i: (bi, hg, 0, 0, 0)),
        pl.BlockSpec((META_PAD, w), lambda bi, hg, i: (0, hg)),
        pl.BlockSpec((NG, PAIR, META_PAD), lambda bi, hg, i: (hg, 0, 0)),
    ]
    out_spec = pl.BlockSpec((1, TQ, w), lambda bi, hg, i: (bi, i, hg))
    return in_specs, out_spec


def _diff_kernel(lq1_ref, lk1_ref, lq2_ref, lk2_ref, g_ref,
                 q_ref, k_ref, vt_ref, km_ref, vtm_ref, o_ref, qq_ref, s_ref, acc_ref, m_ref, l_ref):
    i = pl.program_id(2)
    tq = q_ref.shape[1]
    tk = vt_ref.shape[4]

    for h in range(NG):
        qq_ref[h] = _split_pair(q_ref[0, :, _lanes(h)])
    acc_ref[...] = jnp.zeros_like(acc_ref)
    m_ref[...] = jnp.full_like(m_ref, NEG)
    l_ref[...] = jnp.zeros_like(l_ref)

    def scores(j):
        rows = pl.ds(pl.multiple_of(j * tk, tk), tk)
        return [_nt_dot(k_ref[0, rows, _lanes(h)], qq_ref[h]) for h in range(NG)]

    def update(h, s, vt):
        m = m_ref[h]
        m_new = jnp.maximum(m, jnp.max(s, axis=0, keepdims=True))
        alpha = jnp.exp2(m - m_new)
        p = jnp.exp2(s - m_new)
        l_ref[h] = alpha * l_ref[h] + jnp.sum(p, axis=0, keepdims=True)
        m_ref[h] = m_new
        p = p.astype(BF16)
        if p.shape[0] < vt.shape[1]:
            p = jnp.concatenate([p, jnp.zeros((vt.shape[1] - p.shape[0], p.shape[1]), BF16)], axis=0)
        acc_ref[h] = alpha * acc_ref[h] + _dot(vt, p)

    for h, s in enumerate(scores(0)):
        s_ref[h] = s

    def body(j, _):
        cur = [s_ref[h] for h in range(NG)]
        for h, s in enumerate(scores(j + 1)):
            s_ref[h] = s
        for h in range(NG):
            update(h, cur[h], vt_ref[0, h, j])
        return 0

    lax.fori_loop(0, i, body, 0)

    row = lax.broadcasted_iota(jnp.int32, (tk, 2 * tq), 0)
    col = lax.broadcasted_iota(jnp.int32, (tk, 2 * tq), 1) % tq
    s_meta = [_nt_dot(km_ref[0:N_META, _lanes(h)], qq_ref[h]) for h in range(NG)]
    for h in range(NG):
        update(h, jnp.where(row <= col, s_ref[h], NEG), vt_ref[0, h, i])
    for h in range(NG):
        update(h, s_meta[h], vtm_ref[h])

    lam = (jnp.exp(jnp.sum(lq1_ref[...] * lk1_ref[...], axis=1, keepdims=True))
           - jnp.exp(jnp.sum(lq2_ref[...] * lk2_ref[...], axis=1, keepdims=True)) + LAM_INIT)
    for h in range(NG):
        inv_l = 1.0 / l_ref[h]
        acc = acc_ref[h]
        o = acc[:, :tq] * inv_l[:, :tq] - acc[:, tq:] * (lam * inv_l[:, tq:])
        o = o * lax.rsqrt(jnp.mean(o * o, axis=0, keepdims=True) + EPS)
        o_ref[0, :, _lanes(h)] = (o.T * g_ref[...] * (1.0 - LAM_INIT)).astype(o_ref.dtype)


def _diff_attn(lam_params, g, q, k, vt, km, vtm):
    b, seq, _ = q.shape
    small = pl.BlockSpec((1, HEAD_DIM), lambda bi, hg, i: (0, 0))
    in_specs, out_spec = _attn_specs(b, seq)
    return pl.pallas_call(
        _diff_kernel,
        grid=(b, N_PAIR // NG, seq // TQ),
        in_specs=[small, small, small, small, pl.BlockSpec((1, PAIR), lambda bi, hg, i: (0, 0))] + in_specs,
        out_specs=out_spec,
        out_shape=jax.ShapeDtypeStruct((b, seq, D_MODEL), BF16),
        scratch_shapes=[pltpu.VMEM((NG, 2 * TQ, PAIR), BF16),
                        pltpu.VMEM((NG, TK, 2 * TQ), F32),
                        pltpu.VMEM((NG, PAIR, 2 * TQ), F32),
                        pltpu.VMEM((NG, 1, 2 * TQ), F32),
                        pltpu.VMEM((NG, 1, 2 * TQ), F32)],
        compiler_params=pltpu.CompilerParams(
            dimension_semantics=("arbitrary", "arbitrary", "arbitrary"), vmem_limit_bytes=VMEM_LIMIT),
        name="diff_attn",
    )(*lam_params, g, q, k, vt, km, vtm)


def _sb_kernel(q_ref, k_ref, vt_ref, km_ref, vtm_ref, o_ref, qq_ref, z_ref, acc_ref, c_ref):
    i = pl.program_id(2)
    tq = q_ref.shape[1]
    tk = vt_ref.shape[4]

    ur = lax.broadcasted_iota(jnp.int32, (tk, tk), 0)
    uc = lax.broadcasted_iota(jnp.int32, (tk, tk), 1)
    u = jnp.where(uc >= ur, 1.0, 0.0).astype(BF16)

    for h in range(NG):
        qq_ref[h] = _split_pair(q_ref[0, :, _lanes(h)])
    acc_ref[...] = jnp.zeros_like(acc_ref)
    c_ref[...] = jnp.zeros_like(c_ref)

    def logits(j):
        rows = pl.ds(pl.multiple_of(j * tk, tk), tk)
        return [_nt_dot(k_ref[0, rows, _lanes(h)], qq_ref[h]) for h in range(NG)]

    def update(zs, vts, u_blk, visible):
        parts = []
        for z in zs:
            sp = jnp.maximum(z, 0.0) + jnp.log2(1.0 + jnp.exp2(-jnp.abs(z)))
            if visible is not None:
                sp = jnp.where(visible, sp, 0.0)
            hi = sp.astype(BF16)
            parts.append((hi, (sp - hi.astype(F32)).astype(BF16)))
        tots = [_dot(u_blk, hi) + _dot(u_blk, lo) + c_ref[h] for h, (hi, lo) in enumerate(parts)]
        weights = []
        for h, (z, tot) in enumerate(zip(zs, tots)):
            a = jnp.exp2(z - tot)
            if visible is not None:
                a = jnp.where(visible, a, 0.0)
            weights.append(a.astype(BF16))
            c_ref[h] = tot[0:1, :]
        for h, (a, vt) in enumerate(zip(weights, vts)):
            acc_ref[h, 0:HEAD_DIM, :] += _dot(vt[0:HEAD_DIM, :], a[:, :tq])
            acc_ref[h, HEAD_DIM:PAIR, :] += _dot(vt[HEAD_DIM:PAIR, :], a[:, tq:])

    def min_carry():
        cm = c_ref[0]
        for h in range(1, NG):
            cm = jnp.minimum(cm, c_ref[h])
        return jnp.min(cm)

    def sweep(j, visible):
        cur = [z_ref[h] for h in range(NG)]
        for h, z in enumerate(logits(jnp.maximum(j - 1, 0))):
            z_ref[h] = z
        update(cur, [vt_ref[0, h, j] for h in range(NG)], u, visible)

    for h, z in enumerate(logits(i)):
        z_ref[h] = z
    row = lax.broadcasted_iota(jnp.int32, (tk, 2 * tq), 0)
    col = lax.broadcasted_iota(jnp.int32, (tk, 2 * tq), 1) % tq
    sweep(i, row < col)

    def cond(state):
        t, cmin = state
        return jnp.logical_and(t < i, cmin < SB_DONE_BITS)

    def body(state):
        t, _ = state
        sweep(i - 1 - t, None)
        return t + 1, min_carry()

    _, cmin = lax.while_loop(cond, body, (jnp.int32(0), min_carry()))

    @pl.when(cmin < SB_DONE_BITS)
    def _():
        rowm = lax.broadcasted_iota(jnp.int32, (META_PAD, 2 * tq), 0)
        update([_nt_dot(km_ref[:, _lanes(h)], qq_ref[h]) for h in range(NG)],
               [vtm_ref[h] for h in range(NG)], u[:META_PAD, :META_PAD], rowm < N_META)

    for h in range(NG):
        o_ref[0, :, _lanes(h)] = acc_ref[h].T.astype(o_ref.dtype)


def _sb_attn(q, k, vt, km, vtm):
    b, seq, _ = q.shape
    in_specs, out_spec = _attn_specs(b, seq)
    return pl.pallas_call(
        _sb_kernel,
        grid=(b, N_PAIR // NG, seq // TQ),
        in_specs=in_specs,
        out_specs=out_spec,
        out_shape=jax.ShapeDtypeStruct((b, seq, D_MODEL), BF16),
        scratch_shapes=[pltpu.VMEM((NG, 2 * TQ, PAIR), BF16),
                        pltpu.VMEM((NG, TK, 2 * TQ), F32),
                        pltpu.VMEM((NG, PAIR, TQ), F32),
                        pltpu.VMEM((NG, 1, 2 * TQ), F32)],
        compiler_params=pltpu.CompilerParams(
            dimension_semantics=("arbitrary", "arbitrary", "arbitrary"), vmem_limit_bytes=VMEM_LIMIT),
        name="sb_attn",
    )(q, k, vt, km, vtm)


def _mix_kernel(x_ref, oa_ref, ob_ref, ga_ref, gb_ref, wa_ref, wb_ref, wo_ref, g_ref, h_ref, hm_ref):
    ua = _dot(oa_ref[...], wa_ref[...])
    ub = _dot(ob_ref[...], wb_ref[...])
    merged = jax.nn.sigmoid(ga_ref[...]) * ua + jax.nn.sigmoid(gb_ref[...]) * ub
    h = x_ref[...] + _dot(merged.astype(BF16), wo_ref[...])
    h_ref[...] = h
    r = lax.rsqrt(jnp.mean(h * h, axis=-1, keepdims=True) + EPS)
    hm_ref[...] = (h * r * g_ref[...]).astype(BF16)


def _mix_out(x2d, oa, ob, ga, gb, wa, wb, wo, g, *, tm):
    m = x2d.shape[0]
    row = pl.BlockSpec((tm, D_MODEL), lambda i: (i, 0))
    wspec = pl.BlockSpec((D_MODEL, D_MODEL), lambda i: (0, 0))
    return pl.pallas_call(
        _mix_kernel,
        grid=(m // tm,),
        in_specs=[row, row, row, row, row, wspec, wspec, wspec, pl.BlockSpec((1, D_MODEL), lambda i: (0, 0))],
        out_specs=[row, row],
        out_shape=[jax.ShapeDtypeStruct((m, D_MODEL), F32), jax.ShapeDtypeStruct((m, D_MODEL), BF16)],
        compiler_params=pltpu.CompilerParams(
            dimension_semantics=("arbitrary",), vmem_limit_bytes=VMEM_LIMIT),
        name="mix_out",
    )(x2d, oa, ob, ga, gb, wa, wb, wo, g)


def _mlp_kernel(hm_ref, h_ref, wu_ref, wd_ref, g_ref, o_ref, acc_ref):
    f = pl.program_id(1)

    @pl.when(f == 0)
    def _():
        acc_ref[...] = jnp.zeros_like(acc_ref)

    up = jnp.maximum(_dot(hm_ref[...], wu_ref[...]), 0.0)
    acc_ref[...] += _dot((up * up).astype(BF16), wd_ref[...])

    @pl.when(f == pl.num_programs(1) - 1)
    def _():
        h = h_ref[...] + acc_ref[...]
        r = lax.rsqrt(jnp.mean(h * h, axis=-1, keepdims=True) + EPS)
        o_ref[...] = h * r * g_ref[...]


def _mlp(hm, h, wu, wd, g, *, tm, tf):
    m = h.shape[0]
    row = pl.BlockSpec((tm, D_MODEL), lambda i, f: (i, 0))
    return pl.pallas_call(
        _mlp_kernel,
        grid=(m // tm, D_FF // tf),
        in_specs=[row, row,
                  pl.BlockSpec((D_MODEL, tf), lambda i, f: (0, f)),
                  pl.BlockSpec((tf, D_MODEL), lambda i, f: (f, 0)),
                  pl.BlockSpec((1, D_MODEL), lambda i, f: (0, 0))],
        out_specs=row,
        out_shape=jax.ShapeDtypeStruct((m, D_MODEL), F32),
        scratch_shapes=[pltpu.VMEM((tm, D_MODEL), F32)],
        compiler_params=pltpu.CompilerParams(
            dimension_semantics=("arbitrary", "arbitrary"), vmem_limit_bytes=VMEM_LIMIT),
        name="mlp",
    )(hm, h, wu, wd, g)


def kernel(x, meta_tokens, norm_mix_g, w_in, lambda_q1, lambda_k1, lambda_q2, lambda_k2, subln_g,
           w_out_a, w_out_b, w_out, norm_mlp_g, w_up, w_down, norm_final_g):
    b, seq, d = x.shape
    assert d == D_MODEL and seq % TQ == 0 and TQ == TK
    assert w_in.shape == (1, D_MODEL, N_SEG * D_MODEL)
    x2d = x.reshape(b * seq, d)
    g_mix = norm_mix_g[0].reshape(1, d)
    w_in_b = w_in[0].astype(BF16)

    cos_f, sin_f = _rope_tables(jnp.arange(N_META, N_META + seq, dtype=jnp.int32))
    cos_f = jnp.tile(cos_f, (b, 1))
    sin_f = jnp.tile(sin_f, (b, 1))
    qa, ka, vat, qb, kb, vbt, ga, gb = _inproj(
        x2d, g_mix, w_in_b, cos_f, sin_f, rows_per_batch=seq, tm=512, tk=TK)

    meta_pad = jnp.zeros((META_PAD, d), x.dtype).at[:N_META].set(meta_tokens.astype(x.dtype))
    cos_m, sin_m = _rope_tables(jnp.arange(META_PAD, dtype=jnp.int32))
    _, kam, vatm, _, kbm, vbtm, _, _ = _inproj(
        meta_pad, g_mix, w_in_b, cos_m, sin_m, rows_per_batch=META_PAD, tm=META_PAD, tk=META_PAD)
    vatm = vatm.reshape(N_PAIR, PAIR, META_PAD)
    vbtm = vbtm.reshape(N_PAIR, PAIR, META_PAD)

    lam_params = [p[0].reshape(1, HEAD_DIM) for p in (lambda_q1, lambda_k1, lambda_q2, lambda_k2)]
    oa = _diff_attn(lam_params, subln_g[0].reshape(1, PAIR),
                    qa.reshape(b, seq, d), ka.reshape(b, seq, d), vat, kam, vatm)
    ob = _sb_attn(qb.reshape(b, seq, d), kb.reshape(b, seq, d), vbt, kbm, vbtm)

    h1, hm = _mix_out(x2d, oa.reshape(b * seq, d), ob.reshape(b * seq, d), ga, gb,
                      w_out_a[0].astype(BF16), w_out_b[0].astype(BF16), w_out[0].astype(BF16),
                      norm_mlp_g[0].reshape(1, d), tm=512)
    out = _mlp(hm, h1, w_up[0].astype(BF16), w_down[0].astype(BF16),
               norm_final_g.reshape(1, d), tm=1024, tf=512)
    return out.reshape(b, seq, d)
```

```python
import functools
import math

import jax
import jax.numpy as jnp
from jax import lax
from jax.experimental import pallas as pl
from jax.experimental.pallas import tpu as pltpu

D_MODEL = 1024
N_META = 16
HEAD_DIM = 64
N_PAIR = 8
PAIR = 2 * HEAD_DIM
D_FF = 4 * D_MODEL
N_SEG = 8
ROPE_THETA = 10000.0
EPS = 1e-6
LAM_INIT = 0.8 - 0.6 * math.exp(-0.3 * 0)

LANES = 128
META_PAD = 128
TQ_DIFF = 512
TQ_SB = 256
TK = 256
NG_DIFF = 4
NG_SB = 8
NEG = -1e30
SB_DONE_BITS = 160.0
Q_SCALE = math.log2(math.e) / math.sqrt(HEAD_DIM)
MXU_COLS = 256
VMEM_LIMIT = 48 * 1024 * 1024
VMEM_LIMIT_WIDE = 56 * 1024 * 1024

F32 = jnp.float32
BF16 = jnp.bfloat16


def _nt_dot(a, b):
    return lax.dot_general(a, b, (((1,), (1,)), ((), ())), preferred_element_type=F32)


def _dot(a, b):
    return jnp.dot(a, b, preferred_element_type=F32)


def _inproj_kernel(x_ref, g_ref, w_ref, cos_ref, sin_ref,
                   qa_ref, ka_ref, vat_ref, qb_ref, kb_ref, vbt_ref, ga_ref, gb_ref,
                   hn_ref, *, tk):
    tm = x_ref.shape[0]
    x = x_ref[...]
    r = lax.rsqrt(jnp.mean(x * x, axis=-1, keepdims=True) + EPS)
    hn_ref[...] = (x * r * g_ref[...]).astype(BF16)

    lane = lax.broadcasted_iota(jnp.int32, (tm, LANES), 1)
    first_half = (lane % HEAD_DIM) < (HEAD_DIM // 2)

    def rope(yc, scale):
        partner = jnp.where(first_half,
                            pltpu.roll(yc, LANES - HEAD_DIM // 2, 1),
                            pltpu.roll(yc, HEAD_DIM // 2, 1))
        out = yc * cos_ref[...] + partner * sin_ref[...]
        return out * scale if scale != 1.0 else out

    def transposed_store(o_ref, h, yc):
        t = yc.T.astype(o_ref.dtype)
        for c in range(tm // tk):
            o_ref[0, h, c] = t[:, c * tk:(c + 1) * tk]

    per_seg = D_MODEL // MXU_COLS
    for chunk in range(N_SEG * per_seg):
        seg, part = divmod(chunk, per_seg)
        y = _dot(hn_ref[...], w_ref[:, chunk * MXU_COLS:(chunk + 1) * MXU_COLS])
        for sub in range(MXU_COLS // LANES):
            yc = y[:, sub * LANES:(sub + 1) * LANES]
            blk = part * (MXU_COLS // LANES) + sub
            cols = slice(blk * LANES, (blk + 1) * LANES)
            if seg == 0:
                qa_ref[:, cols] = rope(yc, Q_SCALE).astype(qa_ref.dtype)
            elif seg == 1:
                ka_ref[:, cols] = rope(yc, 1.0).astype(ka_ref.dtype)
            elif seg == 2:
                transposed_store(vat_ref, blk, yc)
            elif seg == 3:
                qb_ref[:, cols] = (yc * Q_SCALE).astype(qb_ref.dtype)
            elif seg == 4:
                kb_ref[:, cols] = yc.astype(kb_ref.dtype)
            elif seg == 5:
                transposed_store(vbt_ref, blk, yc)
            elif seg == 6:
                ga_ref[:, cols] = yc
            else:
                gb_ref[:, cols] = yc


def _inproj(x2d, g, w_bf16, cos, sin, *, rows_per_batch, tm, tk):
    m = x2d.shape[0]
    nb = m // rows_per_batch
    tiles_per_batch = rows_per_batch // tm
    nblk = rows_per_batch // tk
    row_block = pl.BlockSpec((tm, D_MODEL), lambda i: (i, 0))
    vt_block = pl.BlockSpec((1, N_PAIR, tm // tk, LANES, tk),
                            lambda i: (i // tiles_per_batch, 0, i % tiles_per_batch, 0, 0))
    nat = jax.ShapeDtypeStruct((m, D_MODEL), BF16)
    vt = jax.ShapeDtypeStruct((nb, N_PAIR, nblk, LANES, tk), BF16)
    gate = jax.ShapeDtypeStruct((m, D_MODEL), F32)
    return pl.pallas_call(
        functools.partial(_inproj_kernel, tk=tk),
        grid=(m // tm,),
        in_specs=[
            row_block,
            pl.BlockSpec((1, D_MODEL), lambda i: (0, 0)),
            pl.BlockSpec((D_MODEL, N_SEG * D_MODEL), lambda i: (0, 0), pipeline_mode=pl.Buffered(1)),
            pl.BlockSpec((tm, LANES), lambda i: (i, 0)),
            pl.BlockSpec((tm, LANES), lambda i: (i, 0)),
        ],
        out_specs=[row_block, row_block, vt_block, row_block, row_block, vt_block, row_block, row_block],
        out_shape=[nat, nat, vt, nat, nat, vt, gate, gate],
        scratch_shapes=[pltpu.VMEM((tm, D_MODEL), BF16)],
        compiler_params=pltpu.CompilerParams(
            dimension_semantics=("arbitrary",), vmem_limit_bytes=VMEM_LIMIT_WIDE),
        name="inproj",
    )(x2d, g, w_bf16, cos, sin)


def _rope_tables(pos):
    half = HEAD_DIM // 2
    inv = ROPE_THETA ** (-jnp.arange(0, HEAD_DIM, 2, dtype=F32) / HEAD_DIM)
    ang = pos.astype(F32)[:, None] * inv[None, :]
    cos, sin = jnp.cos(ang), jnp.sin(ang)
    reps = LANES // HEAD_DIM
    cos_t = jnp.tile(jnp.concatenate([cos, cos], axis=-1), (1, reps))
    sin_t = jnp.tile(jnp.concatenate([-sin, sin], axis=-1), (1, reps))
    return cos_t, sin_t


def _split_pair(q):
    lane = lax.broadcasted_iota(jnp.int32, q.shape, 1)
    zero = jnp.zeros_like(q)
    return jnp.concatenate([jnp.where(lane < HEAD_DIM, q, zero),
                            jnp.where(lane >= HEAD_DIM, q, zero)], axis=0)


def _lanes(h):
    return slice(h * PAIR, (h + 1) * PAIR)


def _attn_specs(b, seq, tq, ng, kv_buffers):
    w = ng * PAIR
    mode = dict(pipeline_mode=pl.Buffered(kv_buffers))
    in_specs = [
        pl.BlockSpec((1, tq, w), lambda bi, hg, i: (bi, i, hg)),
        pl.BlockSpec((1, seq, w), lambda bi, hg, i: (bi, 0, hg), **mode),
        pl.BlockSpec((1, ng, seq // TK, PAIR, TK), lambda bi, hg, i: (bi, hg, 0, 0, 0), **mode),
        pl.BlockSpec((META_PAD, w), lambda bi, hg, i: (0, hg)),
        pl.BlockSpec((ng, PAIR, META_PAD), lambda bi, hg, i: (hg, 0, 0)),
    ]
    out_spec = pl.BlockSpec((1, tq, w), lambda bi, hg, i: (bi, i, hg))
    return in_specs, out_spec


def _diff_kernel(lq1_ref, lk1_ref, lq2_ref, lk2_ref, g_ref,
                 q_ref, k_ref, vt_ref, km_ref, vtm_ref, o_ref, qq_ref, s_ref, acc_ref, m_ref, l_ref):
    i = pl.program_id(2)
    tq = q_ref.shape[1]
    tk = vt_ref.shape[4]
    ng = qq_ref.shape[0]
    nd = tq // tk

    for h in range(ng):
        qq_ref[h] = _split_pair(q_ref[0, :, _lanes(h)])
    acc_ref[...] = jnp.zeros_like(acc_ref)
    m_ref[...] = jnp.full_like(m_ref, NEG)
    l_ref[...] = jnp.zeros_like(l_ref)

    def stage_scores(j, slot):
        rows = pl.ds(pl.multiple_of(j * tk, tk), tk)
        for h in range(ng):
            s_ref[slot, h] = _nt_dot(k_ref[0, rows, _lanes(h)], qq_ref[h])

    def update(h, load_s, vt):
        m = m_ref[h]
        m_new = jnp.maximum(m, jnp.max(load_s(), axis=0, keepdims=True))
        alpha = jnp.exp2(m - m_new)
        p = jnp.exp2(load_s() - m_new)
        l_ref[h] = alpha * l_ref[h] + jnp.sum(p, axis=0, keepdims=True)
        m_ref[h] = m_new
        p = p.astype(BF16)
        if p.shape[0] < vt.shape[1]:
            p = jnp.concatenate([p, jnp.zeros((vt.shape[1] - p.shape[0], p.shape[1]), BF16)], axis=0)
        acc_ref[h] = alpha * acc_ref[h] + _dot(vt, p)

    assert nd == 2
    stage_scores(0, 0)

    def body(jj, _):
        for slot in range(2):
            j = 2 * jj + slot
            stage_scores(j + 1, 1 - slot)
            for h in range(ng):
                update(h, lambda: s_ref[slot, h], vt_ref[0, h, j])
        return 0

    lax.fori_loop(0, i, body, 0)

    row = lax.broadcasted_iota(jnp.int32, (tk, 2 * tq), 0)
    col = lax.broadcasted_iota(jnp.int32, (tk, 2 * tq), 1) % tq
    s_meta = [_nt_dot(km_ref[0:N_META, _lanes(h)], qq_ref[h]) for h in range(ng)]
    for d in range(nd):
        slot = d
        if d + 1 < nd:
            stage_scores(nd * i + d + 1, 1 - slot)
        for h in range(ng):
            s_ref[slot, h] = jnp.where(row + d * tk <= col, s_ref[slot, h], NEG)
        for h in range(ng):
            update(h, lambda: s_ref[slot, h], vt_ref[0, h, nd * i + d])
    for h in range(ng):
        update(h, lambda: s_meta[h], vtm_ref[h])

    lam = (jnp.exp(jnp.sum(lq1_ref[...] * lk1_ref[...], axis=1, keepdims=True))
           - jnp.exp(jnp.sum(lq2_ref[...] * lk2_ref[...], axis=1, keepdims=True)) + LAM_INIT)
    for h in range(ng):
        inv_l = 1.0 / l_ref[h]
        acc = acc_ref[h]
        o = acc[:, :tq] * inv_l[:, :tq] - acc[:, tq:] * (lam * inv_l[:, tq:])
        o = o * lax.rsqrt(jnp.mean(o * o, axis=0, keepdims=True) + EPS)
        o_ref[0, :, _lanes(h)] = (o.T * g_ref[...] * (1.0 - LAM_INIT)).astype(o_ref.dtype)


def _diff_attn(lam_params, g, q, k, vt, km, vtm):
    b, seq, _ = q.shape
    small = pl.BlockSpec((1, HEAD_DIM), lambda bi, hg, i: (0, 0))
    tq, ng = TQ_DIFF, NG_DIFF
    in_specs, out_spec = _attn_specs(b, seq, tq, ng, kv_buffers=2)
    return pl.pallas_call(
        _diff_kernel,
        grid=(b, N_PAIR // ng, seq // tq),
        in_specs=[small, small, small, small, pl.BlockSpec((1, PAIR), lambda bi, hg, i: (0, 0))] + in_specs,
        out_specs=out_spec,
        out_shape=jax.ShapeDtypeStruct((b, seq, D_MODEL), BF16),
        scratch_shapes=[pltpu.VMEM((ng, 2 * tq, PAIR), BF16),
                        pltpu.VMEM((2, ng, TK, 2 * tq), F32),
                        pltpu.VMEM((ng, PAIR, 2 * tq), F32),
                        pltpu.VMEM((ng, 1, 2 * tq), F32),
                        pltpu.VMEM((ng, 1, 2 * tq), F32)],
        compiler_params=pltpu.CompilerParams(
            dimension_semantics=("arbitrary", "arbitrary", "arbitrary"), vmem_limit_bytes=VMEM_LIMIT),
        name="diff_attn",
    )(*lam_params, g, q, k, vt, km, vtm)


def _sb_kernel(q_ref, k_ref, vt_ref, km_ref, vtm_ref, o_ref, qq_ref, z_ref, acc_ref, c_ref):
    i = pl.program_id(2)
    tq = q_ref.shape[1]
    tk = vt_ref.shape[4]
    ng = qq_ref.shape[0]

    ur = lax.broadcasted_iota(jnp.int32, (tk, tk), 0)
    uc = lax.broadcasted_iota(jnp.int32, (tk, tk), 1)
    u = jnp.where(uc >= ur, 1.0, 0.0).astype(BF16)

    for h in range(ng):
        qq_ref[h] = _split_pair(q_ref[0, :, _lanes(h)])
    acc_ref[...] = jnp.zeros_like(acc_ref)
    c_ref[...] = jnp.zeros_like(c_ref)

    def stage_logits(j, slot):
        rows = pl.ds(pl.multiple_of(j * tk, tk), tk)
        for h in range(ng):
            z_ref[slot, h] = _nt_dot(k_ref[0, rows, _lanes(h)], qq_ref[h])

    def update(load_z, vts, u_blk, visible):
        parts = []
        for h in range(ng):
            z = load_z(h)
            sp = jnp.maximum(z, 0.0) + jnp.log2(1.0 + jnp.exp2(-jnp.abs(z)))
            if visible is not None:
                sp = jnp.where(visible, sp, 0.0)
            hi = sp.astype(BF16)
            parts.append((hi, (sp - hi.astype(F32)).astype(BF16)))
        tots = [_dot(u_blk, hi) + _dot(u_blk, lo) + c_ref[h] for h, (hi, lo) in enumerate(parts)]
        weights = []
        for h, tot in enumerate(tots):
            a = jnp.exp2(load_z(h) - tot)
            if visible is not None:
                a = jnp.where(visible, a, 0.0)
            weights.append(a.astype(BF16))
            c_ref[h] = tot[0:1, :]
        for h, (a, vt) in enumerate(zip(weights, vts)):
            acc_ref[h, 0:HEAD_DIM, :] += _dot(vt[0:HEAD_DIM, :], a[:, :tq])
            acc_ref[h, HEAD_DIM:PAIR, :] += _dot(vt[HEAD_DIM:PAIR, :], a[:, tq:])

    def min_carry():
        cm = c_ref[0]
        for h in range(1, ng):
            cm = jnp.minimum(cm, c_ref[h])
        return jnp.min(cm)

    def sweep(j, slot, visible):
        update(lambda h: z_ref[slot, h], [vt_ref[0, h, j] for h in range(ng)], u, visible)

    prev = jnp.maximum(i - 1, 0)
    stage_logits(i, 0)
    stage_logits(prev, 1)
    row = lax.broadcasted_iota(jnp.int32, (tk, 2 * tq), 0)
    col = lax.broadcasted_iota(jnp.int32, (tk, 2 * tq), 1) % tq
    sweep(i, 0, row < col)

    @pl.when(jnp.logical_and(i >= 1, min_carry() < SB_DONE_BITS))
    def _():
        sweep(prev, 1, None)

    def cond(state):
        t, cmin = state
        return jnp.logical_and(t <= i, cmin < SB_DONE_BITS)

    def body(state):
        t, _ = state
        stage_logits(i - t, 0)
        sweep(i - t, 0, None)
        return t + 1, min_carry()

    _, cmin = lax.while_loop(cond, body, (jnp.int32(2), min_carry()))

    @pl.when(cmin < SB_DONE_BITS)
    def _():
        rowm = lax.broadcasted_iota(jnp.int32, (META_PAD, 2 * tq), 0)
        zm = [_nt_dot(km_ref[:, _lanes(h)], qq_ref[h]) for h in range(ng)]
        update(lambda h: zm[h], [vtm_ref[h] for h in range(ng)], u[:META_PAD, :META_PAD], rowm < N_META)

    for h in range(ng):
        o_ref[0, :, _lanes(h)] = acc_ref[h].T.astype(o_ref.dtype)


def _sb_attn(q, k, vt, km, vtm):
    b, seq, _ = q.shape
    tq, ng = TQ_SB, NG_SB
    in_specs, out_spec = _attn_specs(b, seq, tq, ng, kv_buffers=1)
    return pl.pallas_call(
        _sb_kernel,
        grid=(b, N_PAIR // ng, seq // tq),
        in_specs=in_specs,
        out_specs=out_spec,
        out_shape=jax.ShapeDtypeStruct((b, seq, D_MODEL), BF16),
        scratch_shapes=[pltpu.VMEM((ng, 2 * tq, PAIR), BF16),
                        pltpu.VMEM((2, ng, TK, 2 * tq), F32),
                        pltpu.VMEM((ng, PAIR, tq), F32),
                        pltpu.VMEM((ng, 1, 2 * tq), F32)],
        compiler_params=pltpu.CompilerParams(
            dimension_semantics=("arbitrary", "arbitrary", "arbitrary"), vmem_limit_bytes=VMEM_LIMIT),
        name="sb_attn",
    )(q, k, vt, km, vtm)


def _tail_kernel(x_ref, oa_ref, ob_ref, ga_ref, gb_ref, wa_ref, wb_ref, wo_ref, gm_ref,
                 wu_ref, wd_ref, gf_ref, o_ref, *, tf):
    ua = _dot(oa_ref[...], wa_ref[...])
    ub = _dot(ob_ref[...], wb_ref[...])
    merged = jax.nn.sigmoid(ga_ref[...]) * ua + jax.nn.sigmoid(gb_ref[...]) * ub
    h = x_ref[...] + _dot(merged.astype(BF16), wo_ref[...])
    r = lax.rsqrt(jnp.mean(h * h, axis=-1, keepdims=True) + EPS)
    hm = (h * r * gm_ref[...]).astype(BF16)
    for c in range(D_FF // tf):
        up = jnp.maximum(_dot(hm, wu_ref[:, c * tf:(c + 1) * tf]), 0.0)
        h = h + _dot((up * up).astype(BF16), wd_ref[c * tf:(c + 1) * tf, :])
    r = lax.rsqrt(jnp.mean(h * h, axis=-1, keepdims=True) + EPS)
    o_ref[...] = h * r * gf_ref[...]


def _tail(x2d, oa, ob, ga, gb, wa, wb, wo, gm, wu, wd, gf, *, tm, tf):
    m = x2d.shape[0]
    row = pl.BlockSpec((tm, D_MODEL), lambda i: (i, 0))
    vec = pl.BlockSpec((1, D_MODEL), lambda i: (0, 0))

    def resident(shape):
        return pl.BlockSpec(shape, lambda i: (0, 0), pipeline_mode=pl.Buffered(1))

    return pl.pallas_call(
        functools.partial(_tail_kernel, tf=tf),
        grid=(m // tm,),
        in_specs=[row, row, row, row, row,
                  resident((D_MODEL, D_MODEL)), resident((D_MODEL, D_MODEL)), resident((D_MODEL, D_MODEL)), vec,
                  resident((D_MODEL, D_FF)), resident((D_FF, D_MODEL)), vec],
        out_specs=row,
        out_shape=jax.ShapeDtypeStruct((m, D_MODEL), F32),
        compiler_params=pltpu.CompilerParams(
            dimension_semantics=("arbitrary",), vmem_limit_bytes=VMEM_LIMIT_WIDE),
        name="tail",
    )(x2d, oa, ob, ga, gb, wa, wb, wo, gm, wu, wd, gf)


def kernel(x, meta_tokens, norm_mix_g, w_in, lambda_q1, lambda_k1, lambda_q2, lambda_k2, subln_g,
           w_out_a, w_out_b, w_out, norm_mlp_g, w_up, w_down, norm_final_g):
    b, seq, d = x.shape
    assert d == D_MODEL and seq % TQ_DIFF == 0 and TQ_DIFF % TK == 0 and TQ_SB == TK
    assert w_in.shape == (1, D_MODEL, N_SEG * D_MODEL)
    x2d = x.reshape(b * seq, d)
    g_mix = norm_mix_g[0].reshape(1, d)
    w_in_b = w_in[0].astype(BF16)

    cos_f, sin_f = _rope_tables(jnp.arange(N_META, N_META + seq, dtype=jnp.int32))
    cos_f = jnp.tile(cos_f, (b, 1))
    sin_f = jnp.tile(sin_f, (b, 1))
    qa, ka, vat, qb, kb, vbt, ga, gb = _inproj(
        x2d, g_mix, w_in_b, cos_f, sin_f, rows_per_batch=seq, tm=512, tk=TK)

    meta_pad = jnp.zeros((META_PAD, d), x.dtype).at[:N_META].set(meta_tokens.astype(x.dtype))
    cos_m, sin_m = _rope_tables(jnp.arange(META_PAD, dtype=jnp.int32))
    _, kam, vatm, _, kbm, vbtm, _, _ = _inproj(
        meta_pad, g_mix, w_in_b, cos_m, sin_m, rows_per_batch=META_PAD, tm=META_PAD, tk=META_PAD)
    vatm = vatm.reshape(N_PAIR, PAIR, META_PAD)
    vbtm = vbtm.reshape(N_PAIR, PAIR, META_PAD)

    lam_params = [p[0].reshape(1, HEAD_DIM) for p in (lambda_q1, lambda_k1, lambda_q2, lambda_k2)]
    oa = _diff_attn(lam_params, subln_g[0].reshape(1, PAIR),
                    qa.reshape(b, seq, d), ka.reshape(b, seq, d), vat, kam, vatm)
    ob = _sb_attn(qb.reshape(b, seq, d), kb.reshape(b, seq, d), vbt, kbm, vbtm)

    out = _tail(x2d, oa.reshape(b * seq, d), ob.reshape(b * seq, d), ga, gb,
                w_out_a[0].astype(BF16), w_out_b[0].astype(BF16), w_out[0].astype(BF16),
                norm_mlp_g[0].reshape(1, d), w_up[0].astype(BF16), w_down[0].astype(BF16),
                norm_final_g.reshape(1, d), tm=512, tf=512)
    return out.reshape(b, seq, d)
```

```python
import functools
import math

import jax
import jax.numpy as jnp
from jax import lax
from jax.experimental import pallas as pl
from jax.experimental.pallas import tpu as pltpu

D_MODEL = 1024
N_META = 16
HEAD_DIM = 64
N_PAIR = 8
PAIR = 2 * HEAD_DIM
D_FF = 4 * D_MODEL
N_SEG = 8
ROPE_THETA = 10000.0
EPS = 1e-6
LAM_INIT = 0.8 - 0.6 * math.exp(-0.3 * 0)

LANES = 128
META_PAD = 128
TQ_DIFF = 512
TQ_SB = 256
TK = 256
NG_DIFF = 4
NG_SB = 8
NEG = -1e30
SUM_ROWS = 16
SB_DONE_BITS = 160.0
Q_SCALE = math.log2(math.e) / math.sqrt(HEAD_DIM)
MXU_COLS = 256
VMEM_LIMIT = 48 * 1024 * 1024
VMEM_LIMIT_WIDE = 56 * 1024 * 1024

F32 = jnp.float32
BF16 = jnp.bfloat16


def _nt_dot(a, b):
    return lax.dot_general(a, b, (((1,), (1,)), ((), ())), preferred_element_type=F32)


def _dot(a, b):
    return jnp.dot(a, b, preferred_element_type=F32)


def _inproj_kernel(x_ref, g_ref, w_ref, cos_ref, sin_ref,
                   qa_ref, ka_ref, vat_ref, qb_ref, kb_ref, vbt_ref, ga_ref, gb_ref,
                   hn_ref, *, tk):
    tm = x_ref.shape[0]
    x = x_ref[...]
    r = lax.rsqrt(jnp.mean(x * x, axis=-1, keepdims=True) + EPS)
    hn_ref[...] = (x * r * g_ref[...]).astype(BF16)

    lane = lax.broadcasted_iota(jnp.int32, (tm, LANES), 1)
    first_half = (lane % HEAD_DIM) < (HEAD_DIM // 2)

    def rope(yc, scale):
        partner = jnp.where(first_half,
                            pltpu.roll(yc, LANES - HEAD_DIM // 2, 1),
                            pltpu.roll(yc, HEAD_DIM // 2, 1))
        out = yc * cos_ref[...] + partner * sin_ref[...]
        return out * scale if scale != 1.0 else out

    def transposed_store(o_ref, h, yc):
        t = yc.T.astype(o_ref.dtype)
        for c in range(tm // tk):
            o_ref[0, h, c] = t[:, c * tk:(c + 1) * tk]

    per_seg = D_MODEL // MXU_COLS
    for chunk in range(N_SEG * per_seg):
        seg, part = divmod(chunk, per_seg)
        y = _dot(hn_ref[...], w_ref[:, chunk * MXU_COLS:(chunk + 1) * MXU_COLS])
        for sub in range(MXU_COLS // LANES):
            yc = y[:, sub * LANES:(sub + 1) * LANES]
            blk = part * (MXU_COLS // LANES) + sub
            cols = slice(blk * LANES, (blk + 1) * LANES)
            if seg == 0:
                qa_ref[:, cols] = rope(yc, Q_SCALE).astype(qa_ref.dtype)
            elif seg == 1:
                ka_ref[:, cols] = rope(yc, 1.0).astype(ka_ref.dtype)
            elif seg == 2:
                transposed_store(vat_ref, blk, yc)
            elif seg == 3:
                qb_ref[:, cols] = (yc * Q_SCALE).astype(qb_ref.dtype)
            elif seg == 4:
                kb_ref[:, cols] = yc.astype(kb_ref.dtype)
            elif seg == 5:
                transposed_store(vbt_ref, blk, yc)
            elif seg == 6:
                ga_ref[:, cols] = yc
            else:
                gb_ref[:, cols] = yc


def _inproj(x2d, g, w_bf16, cos, sin, *, rows_per_batch, tm, tk):
    m = x2d.shape[0]
    nb = m // rows_per_batch
    tiles_per_batch = rows_per_batch // tm
    nblk = rows_per_batch // tk
    row_block = pl.BlockSpec((tm, D_MODEL), lambda i: (i, 0))
    vt_block = pl.BlockSpec((1, N_PAIR, tm // tk, LANES, tk),
                            lambda i: (i // tiles_per_batch, 0, i % tiles_per_batch, 0, 0))
    nat = jax.ShapeDtypeStruct((m, D_MODEL), BF16)
    vt = jax.ShapeDtypeStruct((nb, N_PAIR, nblk, LANES, tk), BF16)
    gate = jax.ShapeDtypeStruct((m, D_MODEL), F32)
    return pl.pallas_call(
        functools.partial(_inproj_kernel, tk=tk),
        grid=(m // tm,),
        in_specs=[
            row_block,
            pl.BlockSpec((1, D_MODEL), lambda i: (0, 0)),
            pl.BlockSpec((D_MODEL, N_SEG * D_MODEL), lambda i: (0, 0), pipeline_mode=pl.Buffered(1)),
            pl.BlockSpec((tm, LANES), lambda i: (i, 0)),
            pl.BlockSpec((tm, LANES), lambda i: (i, 0)),
        ],
        out_specs=[row_block, row_block, vt_block, row_block, row_block, vt_block, row_block, row_block],
        out_shape=[nat, nat, vt, nat, nat, vt, gate, gate],
        scratch_shapes=[pltpu.VMEM((tm, D_MODEL), BF16)],
        compiler_params=pltpu.CompilerParams(
            dimension_semantics=("arbitrary",), vmem_limit_bytes=VMEM_LIMIT_WIDE),
        name="inproj",
    )(x2d, g, w_bf16, cos, sin)


def _rope_tables(pos):
    half = HEAD_DIM // 2
    inv = ROPE_THETA ** (-jnp.arange(0, HEAD_DIM, 2, dtype=F32) / HEAD_DIM)
    ang = pos.astype(F32)[:, None] * inv[None, :]
    cos, sin = jnp.cos(ang), jnp.sin(ang)
    reps = LANES // HEAD_DIM
    cos_t = jnp.tile(jnp.concatenate([cos, cos], axis=-1), (1, reps))
    sin_t = jnp.tile(jnp.concatenate([-sin, sin], axis=-1), (1, reps))
    return cos_t, sin_t


def _split_pair(q):
    lane = lax.broadcasted_iota(jnp.int32, q.shape, 1)
    zero = jnp.zeros_like(q)
    return jnp.concatenate([jnp.where(lane < HEAD_DIM, q, zero),
                            jnp.where(lane >= HEAD_DIM, q, zero)], axis=0)


def _lanes(h):
    return slice(h * PAIR, (h + 1) * PAIR)


def _attn_specs(b, seq, tq, ng, kv_buffers):
    w = ng * PAIR
    mode = dict(pipeline_mode=pl.Buffered(kv_buffers))
    in_specs = [
        pl.BlockSpec((1, tq, w), lambda bi, hg, i: (bi, i, hg)),
        pl.BlockSpec((1, seq, w), lambda bi, hg, i: (bi, 0, hg), **mode),
        pl.BlockSpec((1, ng, seq // TK, PAIR, TK), lambda bi, hg, i: (bi, hg, 0, 0, 0), **mode),
        pl.BlockSpec((META_PAD, w), lambda bi, hg, i: (0, hg)),
        pl.BlockSpec((ng, PAIR, META_PAD), lambda bi, hg, i: (hg, 0, 0)),
    ]
    out_spec = pl.BlockSpec((1, tq, w), lambda bi, hg, i: (bi, i, hg))
    return in_specs, out_spec


def _diff_kernel(lq1_ref, lk1_ref, lq2_ref, lk2_ref, g_ref,
                 q_ref, k_ref, vt_ref, km_ref, vtm_ref, o_ref, qq_ref, s_ref, acc_ref, m_ref):
    i = pl.program_id(2)
    tq = q_ref.shape[1]
    tk = vt_ref.shape[4]
    ng = qq_ref.shape[0]
    nd = tq // tk

    for h in range(ng):
        qq_ref[h] = _split_pair(q_ref[0, :, _lanes(h)])
    acc_ref[...] = jnp.zeros_like(acc_ref)
    m_ref[...] = jnp.full_like(m_ref, NEG)

    def stage_scores(j, slot, groups=None):
        rows = pl.ds(pl.multiple_of(j * tk, tk), tk)
        for h in (range(ng) if groups is None else groups):
            s_ref[slot, h] = _nt_dot(k_ref[0, rows, _lanes(h)], qq_ref[h]).astype(BF16)

    def step(j, slot, stage_next, mask=None):
        lead = 1
        if stage_next:
            stage_scores(j + 1, 1 - slot, range(min(lead, ng)))
        for h in range(ng):
            if stage_next and h + lead < ng:
                stage_scores(j + 1, 1 - slot, [h + lead])
            sb = s_ref[slot, h]
            if mask is not None:
                sb = jnp.where(mask, sb, NEG)
            update(h, sb, vt_ref[0, h, j])

    def update(h, sb, vt):
        m = m_ref[h]
        m_new = jnp.maximum(m, jnp.max(sb, axis=0, keepdims=True).astype(F32))
        alpha = jnp.exp2(m - m_new)
        p = jnp.exp2(sb - m_new.astype(BF16))
        m_ref[h] = m_new
        if p.shape[0] < vt.shape[1]:
            p = jnp.concatenate([p, jnp.zeros((vt.shape[1] - p.shape[0], p.shape[1]), BF16)], axis=0)
        vt_aug = jnp.concatenate([vt, jnp.ones((SUM_ROWS, vt.shape[1]), BF16)], axis=0)
        acc_ref[h] = alpha * acc_ref[h] + _dot(vt_aug, p)

    assert nd == 2
    stage_scores(0, 0)

    def body(jj, _):
        for slot in range(2):
            step(2 * jj + slot, slot, True)
        return 0

    lax.fori_loop(0, i, body, 0)

    row = lax.broadcasted_iota(jnp.int32, (tk, 2 * tq), 0)
    col = lax.broadcasted_iota(jnp.int32, (tk, 2 * tq), 1) % tq
    s_meta = [_nt_dot(km_ref[0:N_META, _lanes(h)], qq_ref[h]).astype(BF16) for h in range(ng)]
    for d in range(nd):
        step(nd * i + d, d, d + 1 < nd, row + d * tk <= col)
    for h in range(ng):
        update(h, s_meta[h], vtm_ref[h])

    lam = (jnp.exp(jnp.sum(lq1_ref[...] * lk1_ref[...], axis=1, keepdims=True))
           - jnp.exp(jnp.sum(lq2_ref[...] * lk2_ref[...], axis=1, keepdims=True)) + LAM_INIT)
    for h in range(ng):
        acc = acc_ref[h]
        inv_l = 1.0 / acc[PAIR:PAIR + 1, :]
        o = acc[:PAIR, :tq] * inv_l[:, :tq] - acc[:PAIR, tq:] * (lam * inv_l[:, tq:])
        o = o * lax.rsqrt(jnp.mean(o * o, axis=0, keepdims=True) + EPS)
        o_ref[0, :, _lanes(h)] = (o.T * g_ref[...] * (1.0 - LAM_INIT)).astype(o_ref.dtype)


def _diff_attn(lam_params, g, q, k, vt, km, vtm):
    b, seq, _ = q.shape
    small = pl.BlockSpec((1, HEAD_DIM), lambda bi, hg, i: (0, 0))
    tq, ng = TQ_DIFF, NG_DIFF
    in_specs, out_spec = _attn_specs(b, seq, tq, ng, kv_buffers=2)
    return pl.pallas_call(
        _diff_kernel,
        grid=(b, N_PAIR // ng, seq // tq),
        in_specs=[small, small, small, small, pl.BlockSpec((1, PAIR), lambda bi, hg, i: (0, 0))] + in_specs,
        out_specs=out_spec,
        out_shape=jax.ShapeDtypeStruct((b, seq, D_MODEL), BF16),
        scratch_shapes=[pltpu.VMEM((ng, 2 * tq, PAIR), BF16),
                        pltpu.VMEM((2, ng, TK, 2 * tq), BF16),
                        pltpu.VMEM((ng, PAIR + SUM_ROWS, 2 * tq), F32),
                        pltpu.VMEM((ng, 1, 2 * tq), F32)],
        compiler_params=pltpu.CompilerParams(
            dimension_semantics=("arbitrary", "arbitrary", "arbitrary"), vmem_limit_bytes=VMEM_LIMIT),
        name="diff_attn",
    )(*lam_params, g, q, k, vt, km, vtm)


def _sb_kernel(q_ref, k_ref, vt_ref, km_ref, vtm_ref, o_ref, qq_ref, z_ref, acc_ref, c_ref):
    i = pl.program_id(2)
    tq = q_ref.shape[1]
    tk = vt_ref.shape[4]
    ng = qq_ref.shape[0]

    ur = lax.broadcasted_iota(jnp.int32, (tk, tk), 0)
    uc = lax.broadcasted_iota(jnp.int32, (tk, tk), 1)
    u = jnp.where(uc >= ur, 1.0, 0.0).astype(BF16)

    for h in range(ng):
        qq_ref[h] = _split_pair(q_ref[0, :, _lanes(h)])
    acc_ref[...] = jnp.zeros_like(acc_ref)
    c_ref[...] = jnp.zeros_like(c_ref)

    def stage_logits(j, slot):
        rows = pl.ds(pl.multiple_of(j * tk, tk), tk)
        for h in range(ng):
            z_ref[slot, h] = _nt_dot(k_ref[0, rows, _lanes(h)], qq_ref[h])

    def update(load_z, vts, u_blk, visible):
        parts = []
        for h in range(ng):
            z = load_z(h)
            sp = jnp.maximum(z, 0.0) + jnp.log2(1.0 + jnp.exp2(-jnp.abs(z)))
            if visible is not None:
                sp = jnp.where(visible, sp, 0.0)
            hi = sp.astype(BF16)
            parts.append((hi, (sp - hi.astype(F32)).astype(BF16)))
        tots = [_dot(u_blk, hi) + _dot(u_blk, lo) + c_ref[h] for h, (hi, lo) in enumerate(parts)]
        weights = []
        for h, tot in enumerate(tots):
            a = jnp.exp2(load_z(h) - tot)
            if visible is not None:
                a = jnp.where(visible, a, 0.0)
            weights.append(a.astype(BF16))
            c_ref[h] = tot[0:1, :]
        for h, (a, vt) in enumerate(zip(weights, vts)):
            acc_ref[h, 0:HEAD_DIM, :] += _dot(vt[0:HEAD_DIM, :], a[:, :tq])
            acc_ref[h, HEAD_DIM:PAIR, :] += _dot(vt[HEAD_DIM:PAIR, :], a[:, tq:])

    def min_carry():
        cm = c_ref[0]
        for h in range(1, ng):
            cm = jnp.minimum(cm, c_ref[h])
        return jnp.min(cm)

    def sweep(j, slot, visible):
        update(lambda h: z_ref[slot, h], [vt_ref[0, h, j] for h in range(ng)], u, visible)

    prev = jnp.maximum(i - 1, 0)
    stage_logits(i, 0)
    stage_logits(prev, 1)
    row = lax.broadcasted_iota(jnp.int32, (tk, 2 * tq), 0)
    col = lax.broadcasted_iota(jnp.int32, (tk, 2 * tq), 1) % tq
    sweep(i, 0, row < col)

    @pl.when(jnp.logical_and(i >= 1, min_carry() < SB_DONE_BITS))
    def _():
        sweep(prev, 1, None)

    def cond(state):
        t, cmin = state
        return jnp.logical_and(t <= i, cmin < SB_DONE_BITS)

    def body(state):
        t, _ = state
        stage_logits(i - t, 0)
        sweep(i - t, 0, None)
        return t + 1, min_carry()

    _, cmin = lax.while_loop(cond, body, (jnp.int32(2), min_carry()))

    @pl.when(cmin < SB_DONE_BITS)
    def _():
        rowm = lax.broadcasted_iota(jnp.int32, (META_PAD, 2 * tq), 0)
        zm = [_nt_dot(km_ref[:, _lanes(h)], qq_ref[h]) for h in range(ng)]
        update(lambda h: zm[h], [vtm_ref[h] for h in range(ng)], u[:META_PAD, :META_PAD], rowm < N_META)

    for h in range(ng):
        o_ref[0, :, _lanes(h)] = acc_ref[h].T.astype(o_ref.dtype)


def _sb_attn(q, k, vt, km, vtm):
    b, seq, _ = q.shape
    tq, ng = TQ_SB, NG_SB
    in_specs, out_spec = _attn_specs(b, seq, tq, ng, kv_buffers=1)
    return pl.pallas_call(
        _sb_kernel,
        grid=(b, N_PAIR // ng, seq // tq),
        in_specs=in_specs,
        out_specs=out_spec,
        out_shape=jax.ShapeDtypeStruct((b, seq, D_MODEL), BF16),
        scratch_shapes=[pltpu.VMEM((ng, 2 * tq, PAIR), BF16),
                        pltpu.VMEM((2, ng, TK, 2 * tq), F32),
                        pltpu.VMEM((ng, PAIR, tq), F32),
                        pltpu.VMEM((ng, 1, 2 * tq), F32)],
        compiler_params=pltpu.CompilerParams(
            dimension_semantics=("arbitrary", "arbitrary", "arbitrary"), vmem_limit_bytes=VMEM_LIMIT),
        name="sb_attn",
    )(q, k, vt, km, vtm)


def _tail_kernel(x_ref, oa_ref, ob_ref, ga_ref, gb_ref, wa_ref, wb_ref, wo_ref, gm_ref,
                 wu_ref, wd_ref, gf_ref, o_ref, *, tf):
    ua = _dot(oa_ref[...], wa_ref[...])
    ub = _dot(ob_ref[...], wb_ref[...])
    merged = jax.nn.sigmoid(ga_ref[...]) * ua + jax.nn.sigmoid(gb_ref[...]) * ub
    h = x_ref[...] + _dot(merged.astype(BF16), wo_ref[...])
    r = lax.rsqrt(jnp.mean(h * h, axis=-1, keepdims=True) + EPS)
    hm = (h * r * gm_ref[...]).astype(BF16)
    for c in range(D_FF // tf):
        up = jnp.maximum(_dot(hm, wu_ref[:, c * tf:(c + 1) * tf]), 0.0)
        h = h + _dot((up * up).astype(BF16), wd_ref[c * tf:(c + 1) * tf, :])
    r = lax.rsqrt(jnp.mean(h * h, axis=-1, keepdims=True) + EPS)
    o_ref[...] = h * r * gf_ref[...]


def _tail(x2d, oa, ob, ga, gb, wa, wb, wo, gm, wu, wd, gf, *, tm, tf):
    m = x2d.shape[0]
    row = pl.BlockSpec((tm, D_MODEL), lambda i: (i, 0))
    vec = pl.BlockSpec((1, D_MODEL), lambda i: (0, 0))

    def resident(shape):
        return pl.BlockSpec(shape, lambda i: (0, 0), pipeline_mode=pl.Buffered(1))

    return pl.pallas_call(
        functools.partial(_tail_kernel, tf=tf),
        grid=(m // tm,),
        in_specs=[row, row, row, row, row,
                  resident((D_MODEL, D_MODEL)), resident((D_MODEL, D_MODEL)), resident((D_MODEL, D_MODEL)), vec,
                  resident((D_MODEL, D_FF)), resident((D_FF, D_MODEL)), vec],
        out_specs=row,
        out_shape=jax.ShapeDtypeStruct((m, D_MODEL), F32),
        compiler_params=pltpu.CompilerParams(
            dimension_semantics=("arbitrary",), vmem_limit_bytes=VMEM_LIMIT_WIDE),
        name="tail",
    )(x2d, oa, ob, ga, gb, wa, wb, wo, gm, wu, wd, gf)


def kernel(x, meta_tokens, norm_mix_g, w_in, lambda_q1, lambda_k1, lambda_q2, lambda_k2, subln_g,
           w_out_a, w_out_b, w_out, norm_mlp_g, w_up, w_down, norm_final_g):
    b, seq, d = x.shape
    assert d == D_MODEL and seq % TQ_DIFF == 0 and TQ_DIFF % TK == 0 and TQ_SB == TK
    assert w_in.shape == (1, D_MODEL, N_SEG * D_MODEL)
    x2d = x.reshape(b * seq, d)
    g_mix = norm_mix_g[0].reshape(1, d)
    w_in_b = w_in[0].astype(BF16)

    cos_f, sin_f = _rope_tables(jnp.arange(N_META, N_META + seq, dtype=jnp.int32))
    cos_f = jnp.tile(cos_f, (b, 1))
    sin_f = jnp.tile(sin_f, (b, 1))
    qa, ka, vat, qb, kb, vbt, ga, gb = _inproj(
        x2d, g_mix, w_in_b, cos_f, sin_f, rows_per_batch=seq, tm=512, tk=TK)

    meta_pad = jnp.zeros((META_PAD, d), x.dtype).at[:N_META].set(meta_tokens.astype(x.dtype))
    cos_m, sin_m = _rope_tables(jnp.arange(META_PAD, dtype=jnp.int32))
    _, kam, vatm, _, kbm, vbtm, _, _ = _inproj(
        meta_pad, g_mix, w_in_b, cos_m, sin_m, rows_per_batch=META_PAD, tm=META_PAD, tk=META_PAD)
    vatm = vatm.reshape(N_PAIR, PAIR, META_PAD)
    vbtm = vbtm.reshape(N_PAIR, PAIR, META_PAD)

    lam_params = [p[0].reshape(1, HEAD_DIM) for p in (lambda_q1, lambda_k1, lambda_q2, lambda_k2)]
    oa = _diff_attn(lam_params, subln_g[0].reshape(1, PAIR),
                    qa.reshape(b, seq, d), ka.reshape(b, seq, d), vat, kam, vatm)
    ob = _sb_attn(qb.reshape(b, seq, d), kb.reshape(b, seq, d), vbt, kbm, vbtm)

    out = _tail(x2d, oa.reshape(b * seq, d), ob.reshape(b * seq, d), ga, gb,
                w_out_a[0].astype(BF16), w_out_b[0].astype(BF16), w_out[0].astype(BF16),
                norm_mlp_g[0].reshape(1, d), w_up[0].astype(BF16), w_down[0].astype(BF16),
                norm_final_g.reshape(1, d), tm=512, tf=512)
    return out.reshape(b, seq, d)
```

```python
import functools
import math

import jax
import jax.numpy as jnp
from jax import lax
from jax.experimental import pallas as pl
from jax.experimental.pallas import tpu as pltpu

D_MODEL = 1024
N_META = 16
HEAD_DIM = 64
N_PAIR = 8
PAIR = 2 * HEAD_DIM
D_FF = 4 * D_MODEL
N_SEG = 8
ROPE_THETA = 10000.0
EPS = 1e-6
LAM_INIT = 0.8 - 0.6 * math.exp(-0.3 * 0)

LANES = 128
META_PAD = 128
TQ_DIFF = 512
TQ_SB = 256
TK = 256
NG_DIFF = 4
NG_SB = 8
NEG = -1e30
PARAM_ROWS = 32
SUM_ROWS = 16
SB_DONE_BITS = 160.0
Q_SCALE = math.log2(math.e) / math.sqrt(HEAD_DIM)
MXU_COLS = 256
VMEM_LIMIT = 48 * 1024 * 1024
VMEM_LIMIT_WIDE = 56 * 1024 * 1024

F32 = jnp.float32
BF16 = jnp.bfloat16


def _nt_dot(a, b):
    return lax.dot_general(a, b, (((1,), (1,)), ((), ())), preferred_element_type=F32)


def _dot(a, b):
    return jnp.dot(a, b, preferred_element_type=F32)


def _inproj_kernel(x_ref, g_ref, w_ref, cos_ref, sin_ref,
                   qa_ref, ka_ref, vat_ref, qb_ref, kb_ref, vbt_ref, ga_ref, gb_ref,
                   hn_ref, *, tk):
    tm = x_ref.shape[0]
    x = x_ref[...]
    r = lax.rsqrt(jnp.mean(x * x, axis=-1, keepdims=True) + EPS)
    hn_ref[...] = (x * r * g_ref[...]).astype(BF16)

    lane = lax.broadcasted_iota(jnp.int32, (tm, LANES), 1)
    first_half = (lane % HEAD_DIM) < (HEAD_DIM // 2)

    def rope(yc, scale):
        partner = jnp.where(first_half,
                            pltpu.roll(yc, LANES - HEAD_DIM // 2, 1),
                            pltpu.roll(yc, HEAD_DIM // 2, 1))
        out = yc * cos_ref[...] + partner * sin_ref[...]
        return out * scale if scale != 1.0 else out

    def transposed_store(o_ref, h, yc):
        t = yc.T.astype(o_ref.dtype)
        for c in range(tm // tk):
            o_ref[0, h, c] = t[:, c * tk:(c + 1) * tk]

    per_seg = D_MODEL // MXU_COLS
    for chunk in range(N_SEG * per_seg):
        seg, part = divmod(chunk, per_seg)
        y = _dot(hn_ref[...], w_ref[:, chunk * MXU_COLS:(chunk + 1) * MXU_COLS])
        for sub in range(MXU_COLS // LANES):
            yc = y[:, sub * LANES:(sub + 1) * LANES]
            blk = part * (MXU_COLS // LANES) + sub
            cols = slice(blk * LANES, (blk + 1) * LANES)
            if seg == 0:
                qa_ref[:, cols] = rope(yc, Q_SCALE).astype(qa_ref.dtype)
            elif seg == 1:
                ka_ref[:, cols] = rope(yc, 1.0).astype(ka_ref.dtype)
            elif seg == 2:
                transposed_store(vat_ref, blk, yc)
            elif seg == 3:
                qb_ref[:, cols] = (yc * Q_SCALE).astype(qb_ref.dtype)
            elif seg == 4:
                kb_ref[:, cols] = yc.astype(kb_ref.dtype)
            elif seg == 5:
                transposed_store(vbt_ref, blk, yc)
            elif seg == 6:
                ga_ref[:, cols] = yc
            else:
                gb_ref[:, cols] = yc


def _inproj(x2d, g, w_bf16, cos, sin, *, rows_per_batch, tm, tk):
    m = x2d.shape[0]
    nb = m // rows_per_batch
    tiles_per_batch = rows_per_batch // tm
    nblk = rows_per_batch // tk
    row_block = pl.BlockSpec((tm, D_MODEL), lambda i: (i, 0))
    vt_block = pl.BlockSpec((1, N_PAIR, tm // tk, LANES, tk),
                            lambda i: (i // tiles_per_batch, 0, i % tiles_per_batch, 0, 0))
    nat = jax.ShapeDtypeStruct((m, D_MODEL), BF16)
    vt = jax.ShapeDtypeStruct((nb, N_PAIR, nblk, LANES, tk), BF16)
    gate = jax.ShapeDtypeStruct((m, D_MODEL), F32)
    return pl.pallas_call(
        functools.partial(_inproj_kernel, tk=tk),
        grid=(m // tm,),
        in_specs=[
            row_block,
            pl.BlockSpec((1, D_MODEL), lambda i: (0, 0)),
            pl.BlockSpec((D_MODEL, N_SEG * D_MODEL), lambda i: (0, 0), pipeline_mode=pl.Buffered(1)),
            pl.BlockSpec((tm, LANES), lambda i: (i % tiles_per_batch, 0)),
            pl.BlockSpec((tm, LANES), lambda i: (i % tiles_per_batch, 0)),
        ],
        out_specs=[row_block, row_block, vt_block, row_block, row_block, vt_block, row_block, row_block],
        out_shape=[nat, nat, vt, nat, nat, vt, gate, gate],
        scratch_shapes=[pltpu.VMEM((tm, D_MODEL), BF16)],
        compiler_params=pltpu.CompilerParams(
            dimension_semantics=("arbitrary",), vmem_limit_bytes=VMEM_LIMIT_WIDE),
        name="inproj",
    )(x2d, g, w_bf16, cos, sin)


def _rope_tables(pos):
    half = HEAD_DIM // 2
    inv = ROPE_THETA ** (-jnp.arange(0, HEAD_DIM, 2, dtype=F32) / HEAD_DIM)
    ang = pos.astype(F32)[:, None] * inv[None, :]
    cos, sin = jnp.cos(ang), jnp.sin(ang)
    reps = LANES // HEAD_DIM
    cos_t = jnp.tile(jnp.concatenate([cos, cos], axis=-1), (1, reps))
    sin_t = jnp.tile(jnp.concatenate([-sin, sin], axis=-1), (1, reps))
    return cos_t, sin_t


def _split_pair(q):
    lane = lax.broadcasted_iota(jnp.int32, q.shape, 1)
    zero = jnp.zeros_like(q)
    return jnp.concatenate([jnp.where(lane < HEAD_DIM, q, zero),
                            jnp.where(lane >= HEAD_DIM, q, zero)], axis=0)


def _lanes(h):
    return slice(h * PAIR, (h + 1) * PAIR)


def _attn_specs(b, seq, tq, ng, kv_buffers):
    w = ng * PAIR
    mode = dict(pipeline_mode=pl.Buffered(kv_buffers))
    in_specs = [
        pl.BlockSpec((1, tq, w), lambda bi, hg, i: (bi, i, hg)),
        pl.BlockSpec((1, seq, w), lambda bi, hg, i: (bi, 0, hg), **mode),
        pl.BlockSpec((1, ng, seq // TK, PAIR, TK), lambda bi, hg, i: (bi, hg, 0, 0, 0), **mode),
        pl.BlockSpec((META_PAD, w), lambda bi, hg, i: (0, hg)),
        pl.BlockSpec((ng, PAIR, META_PAD), lambda bi, hg, i: (hg, 0, 0)),
    ]
    out_spec = pl.BlockSpec((1, tq, w), lambda bi, hg, i: (bi, i, hg))
    return in_specs, out_spec


def _split_pair_by_half(q):
    lane = lax.broadcasted_iota(jnp.int32, q.shape, 1)
    zero = jnp.zeros_like(q)
    first = jnp.where(lane < HEAD_DIM, q, zero)
    second = jnp.where(lane >= HEAD_DIM, q, zero)
    half = q.shape[0] // 2
    return jnp.concatenate([first[:half], second[:half], first[half:], second[half:]], axis=0)


def _diff_kernel(par_ref, q_ref, k_ref, vt_ref, km_ref, vtm_ref, o_ref, qq_ref, s_ref, acc_ref, m_ref):
    i = pl.program_id(2)
    tq = q_ref.shape[1]
    tk = vt_ref.shape[4]
    ng = qq_ref.shape[0]
    assert tq == 2 * tk
    every = slice(0, 2 * tq)
    late = slice(tq, 2 * tq)

    for h in range(ng):
        qq_ref[h] = _split_pair_by_half(q_ref[0, :, _lanes(h)])
    acc_ref[...] = jnp.zeros_like(acc_ref)
    m_ref[...] = jnp.full_like(m_ref, NEG)

    def stage_scores(j, slot, groups, cols):
        rows = pl.ds(pl.multiple_of(j * tk, tk), tk)
        for h in groups:
            s_ref[slot, h, :, cols] = _nt_dot(k_ref[0, rows, _lanes(h)], qq_ref[h, cols, :]).astype(BF16)

    def update(h, sb, vt, cols):
        m = m_ref[h, :, cols]
        m_new = jnp.maximum(m, jnp.max(sb, axis=0, keepdims=True).astype(F32))
        alpha = jnp.exp2(m - m_new)
        p = jnp.exp2(sb - m_new.astype(BF16))
        m_ref[h, :, cols] = m_new
        if p.shape[0] < vt.shape[1]:
            p = jnp.concatenate([p, jnp.zeros((vt.shape[1] - p.shape[0], p.shape[1]), BF16)], axis=0)
        vt_aug = jnp.concatenate([vt, jnp.ones((SUM_ROWS, vt.shape[1]), BF16)], axis=0)
        acc_ref[h, :, cols] = alpha * acc_ref[h, :, cols] + _dot(vt_aug, p)

    def step(j, slot, cols=every, next_cols=every, mask=None):
        if next_cols is not None:
            stage_scores(j + 1, 1 - slot, [0], next_cols)
        for h in range(ng):
            if next_cols is not None and h + 1 < ng:
                stage_scores(j + 1, 1 - slot, [h + 1], next_cols)
            sb = s_ref[slot, h, :, cols]
            if mask is not None and sb.shape[1] == tq:
                sb = jnp.where(mask, sb, NEG)
            elif mask is not None:
                sb = jnp.concatenate([jnp.where(mask, sb[:, :tq], NEG), sb[:, tq:]], axis=1)
            update(h, sb, vt_ref[0, h, j], cols)

    stage_scores(0, 0, range(ng), every)

    def body(jj, _):
        step(2 * jj, 0)
        step(2 * jj + 1, 1)
        return 0

    lax.fori_loop(0, i, body, 0)

    row = lax.broadcasted_iota(jnp.int32, (tk, tq), 0)
    col = lax.broadcasted_iota(jnp.int32, (tk, tq), 1) % tk
    causal = row <= col
    s_meta = [_nt_dot(km_ref[0:N_META, _lanes(h)], qq_ref[h]).astype(BF16) for h in range(ng)]
    step(2 * i, 0, next_cols=late, mask=causal)
    step(2 * i + 1, 1, cols=late, next_cols=None, mask=causal)
    for h in range(ng):
        update(h, s_meta[h], vtm_ref[h], every)

    lq1, lk1, lq2, lk2 = (par_ref[r:r + 1, 0:HEAD_DIM] for r in range(4))
    g = par_ref[4:5, :]
    lam = (jnp.exp(jnp.sum(lq1 * lk1, axis=1, keepdims=True))
           - jnp.exp(jnp.sum(lq2 * lk2, axis=1, keepdims=True)) + LAM_INIT)
    for h in range(ng):
        for half in range(2):
            c0 = half * tq
            num = acc_ref[h, 0:PAIR, c0:c0 + tq]
            inv_l = 1.0 / acc_ref[h, PAIR:PAIR + 1, c0:c0 + tq]
            o = num[:, :tk] * inv_l[:, :tk] - num[:, tk:] * (lam * inv_l[:, tk:])
            o = o * lax.rsqrt(jnp.mean(o * o, axis=0, keepdims=True) + EPS)
            o_ref[0, half * tk:(half + 1) * tk, _lanes(h)] = (
                o.T * g * (1.0 - LAM_INIT)).astype(o_ref.dtype)


def _diff_attn(lam_params, g, q, k, vt, km, vtm):
    b, seq, _ = q.shape
    tq, ng = TQ_DIFF, NG_DIFF
    rows = [jnp.pad(p, (0, PAIR - HEAD_DIM)) for p in lam_params] + [g]
    par = jnp.pad(jnp.stack(rows), ((0, PARAM_ROWS - len(rows)), (0, 0)))
    in_specs, out_spec = _attn_specs(b, seq, tq, ng, kv_buffers=2)
    return pl.pallas_call(
        _diff_kernel,
        grid=(b, N_PAIR // ng, seq // tq),
        in_specs=[pl.BlockSpec((PARAM_ROWS, PAIR), lambda bi, hg, i: (0, 0))] + in_specs,
        out_specs=out_spec,
        out_shape=jax.ShapeDtypeStruct((b, seq, D_MODEL), BF16),
        scratch_shapes=[pltpu.VMEM((ng, 2 * tq, PAIR), BF16),
                        pltpu.VMEM((2, ng, TK, 2 * tq), BF16),
                        pltpu.VMEM((ng, PAIR + SUM_ROWS, 2 * tq), F32),
                        pltpu.VMEM((ng, 1, 2 * tq), F32)],
        compiler_params=pltpu.CompilerParams(
            dimension_semantics=("arbitrary", "arbitrary", "arbitrary"), vmem_limit_bytes=VMEM_LIMIT),
        name="diff_attn",
    )(par, q, k, vt, km, vtm)


def _sb_kernel(q_ref, k_ref, vt_ref, km_ref, vtm_ref, o_ref, qq_ref, z_ref, acc_ref, c_ref):
    i = pl.program_id(2)
    tq = q_ref.shape[1]
    tk = vt_ref.shape[4]
    ng = qq_ref.shape[0]

    ur = lax.broadcasted_iota(jnp.int32, (tk, tk), 0)
    uc = lax.broadcasted_iota(jnp.int32, (tk, tk), 1)
    u = jnp.where(uc >= ur, 1.0, 0.0).astype(BF16)

    for h in range(ng):
        qq_ref[h] = _split_pair(q_ref[0, :, _lanes(h)])
    acc_ref[...] = jnp.zeros_like(acc_ref)
    c_ref[...] = jnp.zeros_like(c_ref)

    def stage_logits(j, slot):
        rows = pl.ds(pl.multiple_of(j * tk, tk), tk)
        for h in range(ng):
            z_ref[slot, h] = _nt_dot(k_ref[0, rows, _lanes(h)], qq_ref[h])

    def update(load_z, vts, u_blk, visible):
        parts = []
        for h in range(ng):
            z = load_z(h)
            sp = jnp.maximum(z, 0.0) + jnp.log2(1.0 + jnp.exp2(-jnp.abs(z)))
            if visible is not None:
                sp = jnp.where(visible, sp, 0.0)
            hi = sp.astype(BF16)
            parts.append((hi, (sp - hi.astype(F32)).astype(BF16)))
        tots = [_dot(u_blk, hi) + _dot(u_blk, lo) + c_ref[h] for h, (hi, lo) in enumerate(parts)]
        weights = []
        for h, tot in enumerate(tots):
            a = jnp.exp2(load_z(h) - tot)
            if visible is not None:
                a = jnp.where(visible, a, 0.0)
            weights.append(a.astype(BF16))
            c_ref[h] = tot[0:1, :]
        for h, (a, vt) in enumerate(zip(weights, vts)):
            acc_ref[h, 0:HEAD_DIM, :] += _dot(vt[0:HEAD_DIM, :], a[:, :tq])
            acc_ref[h, HEAD_DIM:PAIR, :] += _dot(vt[HEAD_DIM:PAIR, :], a[:, tq:])

    def min_carry():
        cm = c_ref[0]
        for h in range(1, ng):
            cm = jnp.minimum(cm, c_ref[h])
        return jnp.min(cm)

    def sweep(j, slot, visible):
        update(lambda h: z_ref[slot, h], [vt_ref[0, h, j] for h in range(ng)], u, visible)

    prev = jnp.maximum(i - 1, 0)
    stage_logits(i, 0)
    stage_logits(prev, 1)
    row = lax.broadcasted_iota(jnp.int32, (tk, 2 * tq), 0)
    col = lax.broadcasted_iota(jnp.int32, (tk, 2 * tq), 1) % tq
    sweep(i, 0, row < col)

    @pl.when(jnp.logical_and(i >= 1, min_carry() < SB_DONE_BITS))
    def _():
        sweep(prev, 1, None)

    def cond(state):
        t, cmin = state
        return jnp.logical_and(t <= i, cmin < SB_DONE_BITS)

    def body(state):
        t, _ = state
        stage_logits(i - t, 0)
        sweep(i - t, 0, None)
        return t + 1, min_carry()

    _, cmin = lax.while_loop(cond, body, (jnp.int32(2), min_carry()))

    @pl.when(cmin < SB_DONE_BITS)
    def _():
        rowm = lax.broadcasted_iota(jnp.int32, (META_PAD, 2 * tq), 0)
        zm = [_nt_dot(km_ref[:, _lanes(h)], qq_ref[h]) for h in range(ng)]
        update(lambda h: zm[h], [vtm_ref[h] for h in range(ng)], u[:META_PAD, :META_PAD], rowm < N_META)

    for h in range(ng):
        o_ref[0, :, _lanes(h)] = acc_ref[h].T.astype(o_ref.dtype)


def _sb_attn(q, k, vt, km, vtm):
    b, seq, _ = q.shape
    tq, ng = TQ_SB, NG_SB
    in_specs, out_spec = _attn_specs(b, seq, tq, ng, kv_buffers=1)
    return pl.pallas_call(
        _sb_kernel,
        grid=(b, N_PAIR // ng, seq // tq),
        in_specs=in_specs,
        out_specs=out_spec,
        out_shape=jax.ShapeDtypeStruct((b, seq, D_MODEL), BF16),
        scratch_shapes=[pltpu.VMEM((ng, 2 * tq, PAIR), BF16),
                        pltpu.VMEM((2, ng, TK, 2 * tq), F32),
                        pltpu.VMEM((ng, PAIR, tq), F32),
                        pltpu.VMEM((ng, 1, 2 * tq), F32)],
        compiler_params=pltpu.CompilerParams(
            dimension_semantics=("arbitrary", "arbitrary", "arbitrary"), vmem_limit_bytes=VMEM_LIMIT),
        name="sb_attn",
    )(q, k, vt, km, vtm)


def _tail_kernel(x_ref, oa_ref, ob_ref, ga_ref, gb_ref, wa_ref, wb_ref, wo_ref, gm_ref,
                 wu_ref, wd_ref, gf_ref, o_ref, *, tf):
    ua = _dot(oa_ref[...], wa_ref[...])
    ub = _dot(ob_ref[...], wb_ref[...])
    merged = jax.nn.sigmoid(ga_ref[...]) * ua + jax.nn.sigmoid(gb_ref[...]) * ub
    h = x_ref[...] + _dot(merged.astype(BF16), wo_ref[...])
    r = lax.rsqrt(jnp.mean(h * h, axis=-1, keepdims=True) + EPS)
    hm = (h * r * gm_ref[...]).astype(BF16)
    for c in range(D_FF // tf):
        up = jnp.maximum(_dot(hm, wu_ref[:, c * tf:(c + 1) * tf]), 0.0)
        h = h + _dot((up * up).astype(BF16), wd_ref[c * tf:(c + 1) * tf, :])
    r = lax.rsqrt(jnp.mean(h * h, axis=-1, keepdims=True) + EPS)
    o_ref[...] = h * r * gf_ref[...]


def _tail(x2d, oa, ob, ga, gb, wa, wb, wo, gm, wu, wd, gf, *, tm, tf):
    m = x2d.shape[0]
    row = pl.BlockSpec((tm, D_MODEL), lambda i: (i, 0))
    vec = pl.BlockSpec((1, D_MODEL), lambda i: (0, 0))

    def resident(shape):
        return pl.BlockSpec(shape, lambda i: (0, 0), pipeline_mode=pl.Buffered(1))

    return pl.pallas_call(
        functools.partial(_tail_kernel, tf=tf),
        grid=(m // tm,),
        in_specs=[row, row, row, row, row,
                  resident((D_MODEL, D_MODEL)), resident((D_MODEL, D_MODEL)), resident((D_MODEL, D_MODEL)), vec,
                  resident((D_MODEL, D_FF)), resident((D_FF, D_MODEL)), vec],
        out_specs=row,
        out_shape=jax.ShapeDtypeStruct((m, D_MODEL), F32),
        compiler_params=pltpu.CompilerParams(
            dimension_semantics=("arbitrary",), vmem_limit_bytes=VMEM_LIMIT_WIDE),
        name="tail",
    )(x2d, oa, ob, ga, gb, wa, wb, wo, gm, wu, wd, gf)


def kernel(x, meta_tokens, norm_mix_g, w_in, lambda_q1, lambda_k1, lambda_q2, lambda_k2, subln_g,
           w_out_a, w_out_b, w_out, norm_mlp_g, w_up, w_down, norm_final_g):
    b, seq, d = x.shape
    assert d == D_MODEL and seq % TQ_DIFF == 0 and TQ_DIFF % TK == 0 and TQ_SB == TK
    assert w_in.shape == (1, D_MODEL, N_SEG * D_MODEL)
    x2d = x.reshape(b * seq, d)
    g_mix = norm_mix_g[0].reshape(1, d)
    w_in_b = w_in[0].astype(BF16)

    cos_f, sin_f = _rope_tables(jnp.arange(N_META, N_META + seq, dtype=jnp.int32))
    qa, ka, vat, qb, kb, vbt, ga, gb = _inproj(
        x2d, g_mix, w_in_b, cos_f, sin_f, rows_per_batch=seq, tm=512, tk=TK)

    meta_pad = jnp.zeros((META_PAD, d), x.dtype).at[:N_META].set(meta_tokens.astype(x.dtype))
    cos_m, sin_m = _rope_tables(jnp.arange(META_PAD, dtype=jnp.int32))
    _, kam, vatm, _, kbm, vbtm, _, _ = _inproj(
        meta_pad, g_mix, w_in_b, cos_m, sin_m, rows_per_batch=META_PAD, tm=META_PAD, tk=META_PAD)
    vatm = vatm.reshape(N_PAIR, PAIR, META_PAD)
    vbtm = vbtm.reshape(N_PAIR, PAIR, META_PAD)

    lam_params = [p[0].astype(F32) for p in (lambda_q1, lambda_k1, lambda_q2, lambda_k2)]
    oa = _diff_attn(lam_params, subln_g[0].astype(F32),
                    qa.reshape(b, seq, d), ka.reshape(b, seq, d), vat, kam, vatm)
    ob = _sb_attn(qb.reshape(b, seq, d), kb.reshape(b, seq, d), vbt, kbm, vbtm)

    out = _tail(x2d, oa.reshape(b * seq, d), ob.reshape(b * seq, d), ga, gb,
                w_out_a[0].astype(BF16), w_out_b[0].astype(BF16), w_out[0].astype(BF16),
                norm_mlp_g[0].reshape(1, d), w_up[0].astype(BF16), w_down[0].astype(BF16),
                norm_final_g.reshape(1, d), tm=512, tf=512)
    return out.reshape(b, seq, d)
```

```python
import functools
import math

import jax
import jax.numpy as jnp
from jax import lax
from jax.experimental import pallas as pl
from jax.experimental.pallas import tpu as pltpu

D_MODEL = 1024
N_META = 16
HEAD_DIM = 64
N_PAIR = 8
PAIR = 2 * HEAD_DIM
D_FF = 4 * D_MODEL
N_SEG = 8
ROPE_THETA = 10000.0
EPS = 1e-6
LAM_INIT = 0.8 - 0.6 * math.exp(-0.3 * 0)

LANES = 128
META_PAD = 128
TQ_DIFF = 512
TQ_SB = 256
TK = 256
NG_DIFF = 4
NG_SB = 8
NEG = -1e30
DIAGONAL = "diagonal"
PARAM_ROWS = 32
SUM_ROWS = 16
SB_DONE_BITS = 160.0
Q_SCALE = math.log2(math.e) / math.sqrt(HEAD_DIM)
MXU_COLS = 256
VMEM_LIMIT = 48 * 1024 * 1024
VMEM_LIMIT_WIDE = 56 * 1024 * 1024

F32 = jnp.float32
BF16 = jnp.bfloat16


def _nt_dot(a, b):
    return lax.dot_general(a, b, (((1,), (1,)), ((), ())), preferred_element_type=F32)


def _dot(a, b):
    return jnp.dot(a, b, preferred_element_type=F32)


def _inproj_kernel(x_ref, g_ref, w_ref, cos_ref, sin_ref,
                   qa_ref, ka_ref, vat_ref, qb_ref, kb_ref, vbt_ref, ga_ref, gb_ref,
                   hn_ref, *, tk):
    tm = x_ref.shape[0]
    x = x_ref[...]
    r = lax.rsqrt(jnp.mean(x * x, axis=-1, keepdims=True) + EPS)
    hn_ref[...] = (x * r * g_ref[...]).astype(BF16)

    lane = lax.broadcasted_iota(jnp.int32, (tm, LANES), 1)
    first_half = (lane % HEAD_DIM) < (HEAD_DIM // 2)

    def rope(yc, scale):
        partner = jnp.where(first_half,
                            pltpu.roll(yc, LANES - HEAD_DIM // 2, 1),
                            pltpu.roll(yc, HEAD_DIM // 2, 1))
        out = yc * cos_ref[...] + partner * sin_ref[...]
        return out * scale if scale != 1.0 else out

    def transposed_store(o_ref, h, yc):
        t = yc.T.astype(o_ref.dtype)
        for c in range(tm // tk):
            o_ref[0, h, c] = t[:, c * tk:(c + 1) * tk]

    per_seg = D_MODEL // MXU_COLS
    for chunk in range(N_SEG * per_seg):
        seg, part = divmod(chunk, per_seg)
        y = _dot(hn_ref[...], w_ref[:, chunk * MXU_COLS:(chunk + 1) * MXU_COLS])
        for sub in range(MXU_COLS // LANES):
            yc = y[:, sub * LANES:(sub + 1) * LANES]
            blk = part * (MXU_COLS // LANES) + sub
            cols = slice(blk * LANES, (blk + 1) * LANES)
            if seg == 0:
                qa_ref[:, cols] = rope(yc, Q_SCALE).astype(qa_ref.dtype)
            elif seg == 1:
                ka_ref[:, cols] = rope(yc, 1.0).astype(ka_ref.dtype)
            elif seg == 2:
                transposed_store(vat_ref, blk, yc)
            elif seg == 3:
                qb_ref[:, cols] = (yc * Q_SCALE).astype(qb_ref.dtype)
            elif seg == 4:
                kb_ref[:, cols] = yc.astype(kb_ref.dtype)
            elif seg == 5:
                transposed_store(vbt_ref, blk, yc)
            elif seg == 6:
                ga_ref[:, cols] = yc
            else:
                gb_ref[:, cols] = yc


def _inproj(x2d, g, w_bf16, cos, sin, *, rows_per_batch, tm, tk):
    m = x2d.shape[0]
    nb = m // rows_per_batch
    tiles_per_batch = rows_per_batch // tm
    nblk = rows_per_batch // tk
    row_block = pl.BlockSpec((tm, D_MODEL), lambda i: (i, 0))
    vt_block = pl.BlockSpec((1, N_PAIR, tm // tk, LANES, tk),
                            lambda i: (i // tiles_per_batch, 0, i % tiles_per_batch, 0, 0))
    nat = jax.ShapeDtypeStruct((m, D_MODEL), BF16)
    vt = jax.ShapeDtypeStruct((nb, N_PAIR, nblk, LANES, tk), BF16)
    gate = jax.ShapeDtypeStruct((m, D_MODEL), F32)
    return pl.pallas_call(
        functools.partial(_inproj_kernel, tk=tk),
        grid=(m // tm,),
        in_specs=[
            row_block,
            pl.BlockSpec((1, D_MODEL), lambda i: (0, 0)),
            pl.BlockSpec((D_MODEL, N_SEG * D_MODEL), lambda i: (0, 0), pipeline_mode=pl.Buffered(1)),
            pl.BlockSpec((tm, LANES), lambda i: (i % tiles_per_batch, 0)),
            pl.BlockSpec((tm, LANES), lambda i: (i % tiles_per_batch, 0)),
        ],
        out_specs=[row_block, row_block, vt_block, row_block, row_block, vt_block, row_block, row_block],
        out_shape=[nat, nat, vt, nat, nat, vt, gate, gate],
        scratch_shapes=[pltpu.VMEM((tm, D_MODEL), BF16)],
        compiler_params=pltpu.CompilerParams(
            dimension_semantics=("arbitrary",), vmem_limit_bytes=VMEM_LIMIT_WIDE),
        name="inproj",
    )(x2d, g, w_bf16, cos, sin)


def _rope_tables(pos):
    half = HEAD_DIM // 2
    inv = ROPE_THETA ** (-jnp.arange(0, HEAD_DIM, 2, dtype=F32) / HEAD_DIM)
    ang = pos.astype(F32)[:, None] * inv[None, :]
    cos, sin = jnp.cos(ang), jnp.sin(ang)
    reps = LANES // HEAD_DIM
    cos_t = jnp.tile(jnp.concatenate([cos, cos], axis=-1), (1, reps))
    sin_t = jnp.tile(jnp.concatenate([-sin, sin], axis=-1), (1, reps))
    return cos_t, sin_t


def _split_pair(q):
    lane = lax.broadcasted_iota(jnp.int32, q.shape, 1)
    zero = jnp.zeros_like(q)
    return jnp.concatenate([jnp.where(lane < HEAD_DIM, q, zero),
                            jnp.where(lane >= HEAD_DIM, q, zero)], axis=0)


def _lanes(h):
    return slice(h * PAIR, (h + 1) * PAIR)


def _attn_specs(b, seq, tq, ng, kv_buffers):
    w = ng * PAIR
    mode = dict(pipeline_mode=pl.Buffered(kv_buffers))
    in_specs = [
        pl.BlockSpec((1, tq, w), lambda bi, hg, i: (bi, i, hg)),
        pl.BlockSpec((1, seq, w), lambda bi, hg, i: (bi, 0, hg), **mode),
        pl.BlockSpec((1, ng, seq // TK, PAIR, TK), lambda bi, hg, i: (bi, hg, 0, 0, 0), **mode),
        pl.BlockSpec((META_PAD, w), lambda bi, hg, i: (0, hg)),
        pl.BlockSpec((ng, PAIR, META_PAD), lambda bi, hg, i: (hg, 0, 0)),
    ]
    out_spec = pl.BlockSpec((1, tq, w), lambda bi, hg, i: (bi, i, hg))
    return in_specs, out_spec


def _split_pair_by_half(q):
    lane = lax.broadcasted_iota(jnp.int32, q.shape, 1)
    zero = jnp.zeros_like(q)
    first = jnp.where(lane < HEAD_DIM, q, zero)
    second = jnp.where(lane >= HEAD_DIM, q, zero)
    half = q.shape[0] // 2
    return jnp.concatenate([first[:half], second[:half], first[half:], second[half:]], axis=0)


def _diff_kernel(par_ref, q_ref, k_ref, vt_ref, km_ref, vtm_ref, o_ref, qq_ref, s_ref, acc_ref, m_ref):
    i = pl.program_id(2)
    tq = q_ref.shape[1]
    tk = vt_ref.shape[4]
    ng = qq_ref.shape[0]
    assert tq == 2 * tk
    every = slice(0, 2 * tq)
    late = slice(tq, 2 * tq)

    for h in range(ng):
        qq_ref[h] = _split_pair_by_half(q_ref[0, :, _lanes(h)])
    acc_ref[...] = jnp.zeros_like(acc_ref)
    m_ref[...] = jnp.full_like(m_ref, NEG)

    def stage_scores(j, slot, groups, cols):
        rows = pl.ds(pl.multiple_of(j * tk, tk), tk)
        for h in groups:
            s_ref[slot, h, :, cols] = _nt_dot(k_ref[0, rows, _lanes(h)], qq_ref[h, cols, :]).astype(BF16)

    def update(h, sb, vt, cols):
        m = m_ref[h, :, cols]
        m_new = jnp.maximum(m, jnp.max(sb, axis=0, keepdims=True).astype(F32))
        alpha = jnp.exp2(m - m_new)
        p = jnp.exp2(sb - m_new.astype(BF16))
        m_ref[h, :, cols] = m_new
        if p.shape[0] < vt.shape[1]:
            p = jnp.concatenate([p, jnp.zeros((vt.shape[1] - p.shape[0], p.shape[1]), BF16)], axis=0)
        vt_aug = jnp.concatenate([vt, jnp.ones((SUM_ROWS, vt.shape[1]), BF16)], axis=0)
        acc_ref[h, :, cols] = alpha * acc_ref[h, :, cols] + _dot(vt_aug, p)

    def step(j, slot, cols=every, next_cols=every, mask=None):
        if next_cols is not None:
            stage_scores(j + 1, 1 - slot, [0], next_cols)
        for h in range(ng):
            if next_cols is not None and h + 1 < ng:
                stage_scores(j + 1, 1 - slot, [h + 1], next_cols)
            for c0 in range(cols.start, cols.stop, tq):
                part = slice(c0, c0 + tq)
                sb = s_ref[slot, h, :, part]
                if mask is not None and c0 == cols.start:
                    sb = jnp.where(mask, sb, NEG)
                update(h, sb, vt_ref[0, h, j], part)

    stage_scores(0, 0, range(ng), every)

    def body(jj, _):
        step(2 * jj, 0)
        step(2 * jj + 1, 1)
        return 0

    lax.fori_loop(0, i, body, 0)

    row = lax.broadcasted_iota(jnp.int32, (tk, tq), 0)
    col = lax.broadcasted_iota(jnp.int32, (tk, tq), 1) % tk
    causal = row <= col
    s_meta = [_nt_dot(km_ref[0:N_META, _lanes(h)], qq_ref[h]).astype(BF16) for h in range(ng)]
    step(2 * i, 0, next_cols=late, mask=causal)
    step(2 * i + 1, 1, cols=late, next_cols=None, mask=causal)
    for h in range(ng):
        update(h, s_meta[h], vtm_ref[h], every)

    lq1, lk1, lq2, lk2 = (par_ref[r:r + 1, 0:HEAD_DIM] for r in range(4))
    g = par_ref[4:5, :]
    lam = (jnp.exp(jnp.sum(lq1 * lk1, axis=1, keepdims=True))
           - jnp.exp(jnp.sum(lq2 * lk2, axis=1, keepdims=True)) + LAM_INIT)
    for h in range(ng):
        for half in range(2):
            c0 = half * tq
            num = acc_ref[h, 0:PAIR, c0:c0 + tq]
            inv_l = 1.0 / acc_ref[h, PAIR:PAIR + 1, c0:c0 + tq]
            o = num[:, :tk] * inv_l[:, :tk] - num[:, tk:] * (lam * inv_l[:, tk:])
            o = o * lax.rsqrt(jnp.mean(o * o, axis=0, keepdims=True) + EPS)
            o_ref[0, half * tk:(half + 1) * tk, _lanes(h)] = (
                o.T * g * (1.0 - LAM_INIT)).astype(o_ref.dtype)


def _diff_attn(lam_params, g, q, k, vt, km, vtm):
    b, seq, _ = q.shape
    tq, ng = TQ_DIFF, NG_DIFF
    rows = [jnp.pad(p, (0, PAIR - HEAD_DIM)) for p in lam_params] + [g]
    par = jnp.pad(jnp.stack(rows), ((0, PARAM_ROWS - len(rows)), (0, 0)))
    in_specs, out_spec = _attn_specs(b, seq, tq, ng, kv_buffers=2)
    return pl.pallas_call(
        _diff_kernel,
        grid=(b, N_PAIR // ng, seq // tq),
        in_specs=[pl.BlockSpec((PARAM_ROWS, PAIR), lambda bi, hg, i: (0, 0))] + in_specs,
        out_specs=out_spec,
        out_shape=jax.ShapeDtypeStruct((b, seq, D_MODEL), BF16),
        scratch_shapes=[pltpu.VMEM((ng, 2 * tq, PAIR), BF16),
                        pltpu.VMEM((2, ng, TK, 2 * tq), BF16),
                        pltpu.VMEM((ng, PAIR + SUM_ROWS, 2 * tq), F32),
                        pltpu.VMEM((ng, 1, 2 * tq), F32)],
        compiler_params=pltpu.CompilerParams(
            dimension_semantics=("arbitrary", "arbitrary", "arbitrary"), vmem_limit_bytes=VMEM_LIMIT),
        name="diff_attn",
    )(par, q, k, vt, km, vtm)


def _sb_kernel(q_ref, k_ref, vt_ref, km_ref, vtm_ref, o_ref, qq_ref, z_ref, acc_ref, c_ref):
    i = pl.program_id(2)
    tq = q_ref.shape[1]
    tk = vt_ref.shape[4]
    ng = qq_ref.shape[0]

    ur = lax.broadcasted_iota(jnp.int32, (tk, tk), 0)
    uc = lax.broadcasted_iota(jnp.int32, (tk, tk), 1)
    u = jnp.where(uc >= ur, 1.0, 0.0).astype(BF16)

    for h in range(ng):
        qq_ref[h] = _split_pair(q_ref[0, :, _lanes(h)])
    acc_ref[...] = jnp.zeros_like(acc_ref)
    c_ref[...] = jnp.zeros_like(c_ref)

    def stage_logits(j, slot, groups):
        rows = pl.ds(pl.multiple_of(j * tk, tk), tk)
        for h in groups:
            z_ref[slot, h] = _nt_dot(k_ref[0, rows, _lanes(h)], qq_ref[h])

    def staged(slot):
        return lambda h: z_ref[slot, h]

    def staging(j, slot):
        done = set()

        def load(h):
            for g in (h, h + 1):
                if g < ng and g not in done:
                    stage_logits(j, slot, [g])
                    done.add(g)
            return z_ref[slot, h]
        return load

    half = tk // 2
    tri = (lax.broadcasted_iota(jnp.int32, (half, half), 0)
           < lax.broadcasted_iota(jnp.int32, (half, half), 1))

    def on_diagonal(fn, *arrays):
        rows_out = []
        for rb in range(2):
            tiles = []
            for cb in range(2 * tq // half):
                qh = cb % 2
                if rb > qh:
                    tiles.append(jnp.zeros((half, half), F32))
                    continue
                t = fn(*(a[rb * half:(rb + 1) * half, cb * half:(cb + 1) * half] for a in arrays))
                tiles.append(jnp.where(tri, t, 0.0) if rb == qh else t)
            rows_out.append(jnp.concatenate(tiles, axis=1))
        return jnp.concatenate(rows_out, axis=0)

    def diagonal_sums(hi, lo):
        def cols(x, qh, rows):
            return jnp.concatenate([x[rows, (2 * g + qh) * half:(2 * g + qh + 1) * half] for g in range(2)], axis=1)

        first_rows, all_rows = slice(0, half), slice(0, tk)
        early = (_dot(u[:half, :half], cols(hi, 0, first_rows))
                 + _dot(u[:half, :half], cols(lo, 0, first_rows)))
        late = _dot(u, cols(hi, 1, all_rows)) + _dot(u, cols(lo, 1, all_rows))
        unseen = jnp.zeros((half, half), F32)
        top = [early[:, :half], late[:half, :half], early[:, half:], late[:half, half:]]
        bottom = [unseen, late[half:, :half], unseen, late[half:, half:]]
        return jnp.concatenate([jnp.concatenate(top, axis=1), jnp.concatenate(bottom, axis=1)], axis=0)

    def softplus2(z):
        return jnp.maximum(z, 0.0) + jnp.log2(1.0 + jnp.exp2(-jnp.abs(z)))

    def update(load_z, vts, u_blk, visible):
        parts = []
        for h in range(ng):
            z = load_z(h)
            if visible is DIAGONAL:
                sp = on_diagonal(softplus2, z)
            else:
                sp = softplus2(z)
                if visible is not None:
                    sp = jnp.where(visible, sp, 0.0)
            hi = sp.astype(BF16)
            parts.append((hi, (sp - hi.astype(F32)).astype(BF16)))
        if visible is DIAGONAL:
            tots = [diagonal_sums(hi, lo) + c_ref[h] for h, (hi, lo) in enumerate(parts)]
        else:
            tots = [_dot(u_blk, hi) + _dot(u_blk, lo) + c_ref[h] for h, (hi, lo) in enumerate(parts)]
        weights = []
        for h, tot in enumerate(tots):
            if visible is DIAGONAL:
                a = on_diagonal(lambda z, t: jnp.exp2(z - t), load_z(h), tot)
            else:
                a = jnp.exp2(load_z(h) - tot)
                if visible is not None:
                    a = jnp.where(visible, a, 0.0)
            weights.append(a.astype(BF16))
            c_ref[h] = tot[0:1, :]
        for h, (a, vt) in enumerate(zip(weights, vts)):
            acc_ref[h, 0:HEAD_DIM, :] += _dot(vt[0:HEAD_DIM, :], a[:, :tq])
            acc_ref[h, HEAD_DIM:PAIR, :] += _dot(vt[HEAD_DIM:PAIR, :], a[:, tq:])

    def min_carry():
        cm = c_ref[0]
        for h in range(1, ng):
            cm = jnp.minimum(cm, c_ref[h])
        return jnp.min(cm)

    def sweep(j, load_z, visible):
        update(load_z, [vt_ref[0, h, j] for h in range(ng)], u, visible)

    stage_logits(i, 0, range(ng))
    sweep(i, staged(0), DIAGONAL)

    @pl.when(jnp.logical_and(i >= 1, min_carry() < SB_DONE_BITS))
    def _():
        sweep(i - 1, staging(i - 1, 1), None)

    def cond(state):
        t, cmin = state
        return jnp.logical_and(t <= i, cmin < SB_DONE_BITS)

    def body(state):
        t, _ = state
        sweep(i - t, staging(i - t, 0), None)
        return t + 1, min_carry()

    _, cmin = lax.while_loop(cond, body, (jnp.int32(2), min_carry()))

    @pl.when(cmin < SB_DONE_BITS)
    def _():
        rowm = lax.broadcasted_iota(jnp.int32, (META_PAD, 2 * tq), 0)
        zm = [_nt_dot(km_ref[:, _lanes(h)], qq_ref[h]) for h in range(ng)]
        update(lambda h: zm[h], [vtm_ref[h] for h in range(ng)], u[:META_PAD, :META_PAD], rowm < N_META)

    for h in range(ng):
        o_ref[0, :, _lanes(h)] = acc_ref[h].T.astype(o_ref.dtype)


def _sb_attn(q, k, vt, km, vtm):
    b, seq, _ = q.shape
    tq, ng = TQ_SB, NG_SB
    in_specs, out_spec = _attn_specs(b, seq, tq, ng, kv_buffers=1)
    return pl.pallas_call(
        _sb_kernel,
        grid=(b, N_PAIR // ng, seq // tq),
        in_specs=in_specs,
        out_specs=out_spec,
        out_shape=jax.ShapeDtypeStruct((b, seq, D_MODEL), BF16),
        scratch_shapes=[pltpu.VMEM((ng, 2 * tq, PAIR), BF16),
                        pltpu.VMEM((2, ng, TK, 2 * tq), F32),
                        pltpu.VMEM((ng, PAIR, tq), F32),
                        pltpu.VMEM((ng, 1, 2 * tq), F32)],
        compiler_params=pltpu.CompilerParams(
            dimension_semantics=("arbitrary", "arbitrary", "arbitrary"), vmem_limit_bytes=VMEM_LIMIT),
        name="sb_attn",
    )(q, k, vt, km, vtm)


def _tail_kernel(x_ref, oa_ref, ob_ref, ga_ref, gb_ref, wa_ref, wb_ref, wo_ref, gm_ref,
                 wu_ref, wd_ref, gf_ref, o_ref, *, tf):
    ua = _dot(oa_ref[...], wa_ref[...])
    ub = _dot(ob_ref[...], wb_ref[...])
    merged = jax.nn.sigmoid(ga_ref[...]) * ua + jax.nn.sigmoid(gb_ref[...]) * ub
    h = x_ref[...] + _dot(merged.astype(BF16), wo_ref[...])
    r = lax.rsqrt(jnp.mean(h * h, axis=-1, keepdims=True) + EPS)
    hm = (h * r * gm_ref[...]).astype(BF16)
    for c in range(D_FF // tf):
        up = jnp.maximum(_dot(hm, wu_ref[:, c * tf:(c + 1) * tf]), 0.0)
        h = h + _dot((up * up).astype(BF16), wd_ref[c * tf:(c + 1) * tf, :])
    r = lax.rsqrt(jnp.mean(h * h, axis=-1, keepdims=True) + EPS)
    o_ref[...] = h * r * gf_ref[...]


def _tail(x2d, oa, ob, ga, gb, wa, wb, wo, gm, wu, wd, gf, *, tm, tf):
    m = x2d.shape[0]
    row = pl.BlockSpec((tm, D_MODEL), lambda i: (i, 0))
    vec = pl.BlockSpec((1, D_MODEL), lambda i: (0, 0))

    def resident(shape):
        return pl.BlockSpec(shape, lambda i: (0, 0), pipeline_mode=pl.Buffered(1))

    return pl.pallas_call(
        functools.partial(_tail_kernel, tf=tf),
        grid=(m // tm,),
        in_specs=[row, row, row, row, row,
                  resident((D_MODEL, D_MODEL)), resident((D_MODEL, D_MODEL)), resident((D_MODEL, D_MODEL)), vec,
                  resident((D_MODEL, D_FF)), resident((D_FF, D_MODEL)), vec],
        out_specs=row,
        out_shape=jax.ShapeDtypeStruct((m, D_MODEL), F32),
        compiler_params=pltpu.CompilerParams(
            dimension_semantics=("arbitrary",), vmem_limit_bytes=VMEM_LIMIT_WIDE),
        name="tail",
    )(x2d, oa, ob, ga, gb, wa, wb, wo, gm, wu, wd, gf)


def kernel(x, meta_tokens, norm_mix_g, w_in, lambda_q1, lambda_k1, lambda_q2, lambda_k2, subln_g,
           w_out_a, w_out_b, w_out, norm_mlp_g, w_up, w_down, norm_final_g):
    b, seq, d = x.shape
    assert d == D_MODEL and seq % TQ_DIFF == 0 and TQ_DIFF % TK == 0 and TQ_SB == TK
    assert w_in.shape == (1, D_MODEL, N_SEG * D_MODEL)
    x2d = x.reshape(b * seq, d)
    g_mix = norm_mix_g[0].reshape(1, d)
    w_in_b = w_in[0].astype(BF16)

    cos_f, sin_f = _rope_tables(jnp.arange(N_META, N_META + seq, dtype=jnp.int32))
    qa, ka, vat, qb, kb, vbt, ga, gb = _inproj(
        x2d, g_mix, w_in_b, cos_f, sin_f, rows_per_batch=seq, tm=512, tk=TK)

    meta_pad = jnp.zeros((META_PAD, d), x.dtype).at[:N_META].set(meta_tokens.astype(x.dtype))
    cos_m, sin_m = _rope_tables(jnp.arange(META_PAD, dtype=jnp.int32))
    _, kam, vatm, _, kbm, vbtm, _, _ = _inproj(
        meta_pad, g_mix, w_in_b, cos_m, sin_m, rows_per_batch=META_PAD, tm=META_PAD, tk=META_PAD)
    vatm = vatm.reshape(N_PAIR, PAIR, META_PAD)
    vbtm = vbtm.reshape(N_PAIR, PAIR, META_PAD)

    lam_params = [p[0].astype(F32) for p in (lambda_q1, lambda_k1, lambda_q2, lambda_k2)]
    oa = _diff_attn(lam_params, subln_g[0].astype(F32),
                    qa.reshape(b, seq, d), ka.reshape(b, seq, d), vat, kam, vatm)
    ob = _sb_attn(qb.reshape(b, seq, d), kb.reshape(b, seq, d), vbt, kbm, vbtm)

    out = _tail(x2d, oa.reshape(b * seq, d), ob.reshape(b * seq, d), ga, gb,
                w_out_a[0].astype(BF16), w_out_b[0].astype(BF16), w_out[0].astype(BF16),
                norm_mlp_g[0].reshape(1, d), w_up[0].astype(BF16), w_down[0].astype(BF16),
                norm_final_g.reshape(1, d), tm=512, tf=512)
    return out.reshape(b, seq, d)
```

```python
import functools
import math

import jax
import jax.numpy as jnp
from jax import lax
from jax.experimental import pallas as pl
from jax.experimental.pallas import tpu as pltpu

D_MODEL = 1024
N_META = 16
HEAD_DIM = 64
N_PAIR = 8
PAIR = 2 * HEAD_DIM
D_FF = 4 * D_MODEL
N_SEG = 8
ROPE_THETA = 10000.0
EPS = 1e-6
LAM_INIT = 0.8 - 0.6 * math.exp(-0.3 * 0)

LANES = 128
META_PAD = 128
TQ_DIFF = 512
TQ_SB = 256
TK = 256
NG_DIFF = 4
NG_SB = 8
NEG = -1e30
DIAGONAL = "diagonal"
PARAM_ROWS = 32
SUM_ROWS = 16
SB_DONE_BITS = 160.0
Q_SCALE = math.log2(math.e) / math.sqrt(HEAD_DIM)
MXU_COLS = 256
VMEM_LIMIT = 48 * 1024 * 1024
VMEM_LIMIT_WIDE = 56 * 1024 * 1024

F32 = jnp.float32
BF16 = jnp.bfloat16


def _nt_dot(a, b):
    return lax.dot_general(a, b, (((1,), (1,)), ((), ())), preferred_element_type=F32)


def _dot(a, b):
    return jnp.dot(a, b, preferred_element_type=F32)


def _inproj_kernel(x_ref, g_ref, w_ref, cos_ref, sin_ref,
                   qa_ref, ka_ref, vat_ref, qb_ref, kb_ref, vbt_ref, ga_ref, gb_ref,
                   hn_ref, *, tk):
    tm = x_ref.shape[0]
    x = x_ref[...]
    r = lax.rsqrt(jnp.mean(x * x, axis=-1, keepdims=True) + EPS)
    hn_ref[...] = (x * r * g_ref[...]).astype(BF16)

    lane = lax.broadcasted_iota(jnp.int32, (tm, LANES), 1)
    first_half = (lane % HEAD_DIM) < (HEAD_DIM // 2)

    def rope(yc, scale):
        partner = jnp.where(first_half,
                            pltpu.roll(yc, LANES - HEAD_DIM // 2, 1),
                            pltpu.roll(yc, HEAD_DIM // 2, 1))
        out = yc * cos_ref[...] + partner * sin_ref[...]
        return out * scale if scale != 1.0 else out

    def transposed_store(o_ref, h, yc):
        t = yc.T.astype(o_ref.dtype)
        for c in range(tm // tk):
            o_ref[0, h, c] = t[:, c * tk:(c + 1) * tk]

    per_seg = D_MODEL // MXU_COLS
    for chunk in range(N_SEG * per_seg):
        seg, part = divmod(chunk, per_seg)
        y = _dot(hn_ref[...], w_ref[:, chunk * MXU_COLS:(chunk + 1) * MXU_COLS])
        for sub in range(MXU_COLS // LANES):
            yc = y[:, sub * LANES:(sub + 1) * LANES]
            blk = part * (MXU_COLS // LANES) + sub
            cols = slice(blk * LANES, (blk + 1) * LANES)
            if seg == 0:
                qa_ref[:, cols] = rope(yc, Q_SCALE).astype(qa_ref.dtype)
            elif seg == 1:
                ka_ref[:, cols] = rope(yc, 1.0).astype(ka_ref.dtype)
            elif seg == 2:
                transposed_store(vat_ref, blk, yc)
            elif seg == 3:
                qb_ref[:, cols] = (yc * Q_SCALE).astype(qb_ref.dtype)
            elif seg == 4:
                kb_ref[:, cols] = yc.astype(kb_ref.dtype)
            elif seg == 5:
                transposed_store(vbt_ref, blk, yc)
            elif seg == 6:
                ga_ref[:, cols] = yc
            else:
                gb_ref[:, cols] = yc


def _inproj(x2d, g, w_bf16, cos, sin, *, rows_per_batch, tm, tk):
    m = x2d.shape[0]
    nb = m // rows_per_batch
    tiles_per_batch = rows_per_batch // tm
    nblk = rows_per_batch // tk
    row_block = pl.BlockSpec((tm, D_MODEL), lambda i: (i, 0))
    vt_block = pl.BlockSpec((1, N_PAIR, tm // tk, LANES, tk),
                            lambda i: (i // tiles_per_batch, 0, i % tiles_per_batch, 0, 0))
    nat = jax.ShapeDtypeStruct((m, D_MODEL), BF16)
    vt = jax.ShapeDtypeStruct((nb, N_PAIR, nblk, LANES, tk), BF16)
    gate = jax.ShapeDtypeStruct((m, D_MODEL), F32)
    return pl.pallas_call(
        functools.partial(_inproj_kernel, tk=tk),
        grid=(m // tm,),
        in_specs=[
            row_block,
            pl.BlockSpec((1, D_MODEL), lambda i: (0, 0)),
            pl.BlockSpec((D_MODEL, N_SEG * D_MODEL), lambda i: (0, 0), pipeline_mode=pl.Buffered(1)),
            pl.BlockSpec((tm, LANES), lambda i: (i % tiles_per_batch, 0)),
            pl.BlockSpec((tm, LANES), lambda i: (i % tiles_per_batch, 0)),
        ],
        out_specs=[row_block, row_block, vt_block, row_block, row_block, vt_block, row_block, row_block],
        out_shape=[nat, nat, vt, nat, nat, vt, gate, gate],
        scratch_shapes=[pltpu.VMEM((tm, D_MODEL), BF16)],
        compiler_params=pltpu.CompilerParams(
            dimension_semantics=("arbitrary",), vmem_limit_bytes=VMEM_LIMIT_WIDE),
        name="inproj",
    )(x2d, g, w_bf16, cos, sin)


def _rope_tables(pos):
    half = HEAD_DIM // 2
    inv = ROPE_THETA ** (-jnp.arange(0, HEAD_DIM, 2, dtype=F32) / HEAD_DIM)
    ang = pos.astype(F32)[:, None] * inv[None, :]
    cos, sin = jnp.cos(ang), jnp.sin(ang)
    reps = LANES // HEAD_DIM
    cos_t = jnp.tile(jnp.concatenate([cos, cos], axis=-1), (1, reps))
    sin_t = jnp.tile(jnp.concatenate([-sin, sin], axis=-1), (1, reps))
    return cos_t, sin_t


def _split_pair(q):
    lane = lax.broadcasted_iota(jnp.int32, q.shape, 1)
    zero = jnp.zeros_like(q)
    return jnp.concatenate([jnp.where(lane < HEAD_DIM, q, zero),
                            jnp.where(lane >= HEAD_DIM, q, zero)], axis=0)


def _lanes(h):
    return slice(h * PAIR, (h + 1) * PAIR)


def _attn_specs(b, seq, tq, ng, kv_buffers):
    w = ng * PAIR
    mode = dict(pipeline_mode=pl.Buffered(kv_buffers))
    in_specs = [
        pl.BlockSpec((1, tq, w), lambda bi, hg, i: (bi, i, hg)),
        pl.BlockSpec((1, seq, w), lambda bi, hg, i: (bi, 0, hg), **mode),
        pl.BlockSpec((1, ng, seq // TK, PAIR, TK), lambda bi, hg, i: (bi, hg, 0, 0, 0), **mode),
        pl.BlockSpec((META_PAD, w), lambda bi, hg, i: (0, hg)),
        pl.BlockSpec((ng, PAIR, META_PAD), lambda bi, hg, i: (hg, 0, 0)),
    ]
    out_spec = pl.BlockSpec((1, tq, w), lambda bi, hg, i: (bi, i, hg))
    return in_specs, out_spec


def _split_pair_by_half(q):
    lane = lax.broadcasted_iota(jnp.int32, q.shape, 1)
    zero = jnp.zeros_like(q)
    first = jnp.where(lane < HEAD_DIM, q, zero)
    second = jnp.where(lane >= HEAD_DIM, q, zero)
    half = q.shape[0] // 2
    return jnp.concatenate([first[:half], second[:half], first[half:], second[half:]], axis=0)


def _diff_kernel(par_ref, q_ref, k_ref, vt_ref, km_ref, vtm_ref, o_ref, qq_ref, s_ref, acc_ref, m_ref):
    i = pl.program_id(2)
    tq = q_ref.shape[1]
    tk = vt_ref.shape[4]
    ng = qq_ref.shape[0]
    assert tq == 2 * tk
    every = slice(0, 2 * tq)
    late = slice(tq, 2 * tq)

    for h in range(ng):
        qq_ref[h] = _split_pair_by_half(q_ref[0, :, _lanes(h)])
    acc_ref[...] = jnp.zeros_like(acc_ref)
    m_ref[...] = jnp.full_like(m_ref, NEG)

    def stage_scores(j, slot, groups, cols):
        rows = pl.ds(pl.multiple_of(j * tk, tk), tk)
        for h in groups:
            s_ref[slot, h, :, cols] = _nt_dot(k_ref[0, rows, _lanes(h)], qq_ref[h, cols, :]).astype(BF16)

    def update(h, sb, vt, cols):
        m = m_ref[h, :, cols]
        m_new = jnp.maximum(m, jnp.max(sb, axis=0, keepdims=True).astype(F32))
        alpha = jnp.exp2(m - m_new)
        p = jnp.exp2(sb - m_new.astype(BF16))
        m_ref[h, :, cols] = m_new
        if p.shape[0] < vt.shape[1]:
            p = jnp.concatenate([p, jnp.zeros((vt.shape[1] - p.shape[0], p.shape[1]), BF16)], axis=0)
        vt_aug = jnp.concatenate([vt, jnp.ones((SUM_ROWS, vt.shape[1]), BF16)], axis=0)
        acc_ref[h, :, cols] = alpha * acc_ref[h, :, cols] + _dot(vt_aug, p)

    def step(j, slot, cols=every, next_cols=every, mask=None):
        if next_cols is not None:
            stage_scores(j + 1, 1 - slot, [0], next_cols)
        for h in range(ng):
            if next_cols is not None and h + 1 < ng:
                stage_scores(j + 1, 1 - slot, [h + 1], next_cols)
            sb = s_ref[slot, h, :, cols]
            if mask is not None and sb.shape[1] == tq:
                sb = jnp.where(mask, sb, NEG)
            elif mask is not None:
                sb = jnp.concatenate([jnp.where(mask, sb[:, :tq], NEG), sb[:, tq:]], axis=1)
            update(h, sb, vt_ref[0, h, j], cols)

    stage_scores(0, 0, range(ng), every)

    def body(jj, _):
        step(2 * jj, 0)
        step(2 * jj + 1, 1)
        return 0

    lax.fori_loop(0, i, body, 0)

    row = lax.broadcasted_iota(jnp.int32, (tk, tq), 0)
    col = lax.broadcasted_iota(jnp.int32, (tk, tq), 1) % tk
    causal = row <= col
    s_meta = [_nt_dot(km_ref[0:N_META, _lanes(h)], qq_ref[h]).astype(BF16) for h in range(ng)]
    step(2 * i, 0, next_cols=late, mask=causal)
    step(2 * i + 1, 1, cols=late, next_cols=None, mask=causal)
    for h in range(ng):
        update(h, s_meta[h], vtm_ref[h], every)

    lq1, lk1, lq2, lk2 = (par_ref[r:r + 1, 0:HEAD_DIM] for r in range(4))
    g = par_ref[4:5, :]
    lam = (jnp.exp(jnp.sum(lq1 * lk1, axis=1, keepdims=True))
           - jnp.exp(jnp.sum(lq2 * lk2, axis=1, keepdims=True)) + LAM_INIT)
    for h in range(ng):
        for half in range(2):
            c0 = half * tq
            num = acc_ref[h, 0:PAIR, c0:c0 + tq]
            inv_l = 1.0 / acc_ref[h, PAIR:PAIR + 1, c0:c0 + tq]
            o = num[:, :tk] * inv_l[:, :tk] - num[:, tk:] * (lam * inv_l[:, tk:])
            o = o * lax.rsqrt(jnp.mean(o * o, axis=0, keepdims=True) + EPS)
            o_ref[0, half * tk:(half + 1) * tk, _lanes(h)] = (
                o.T * g * (1.0 - LAM_INIT)).astype(o_ref.dtype)


def _diff_attn(lam_params, g, q, k, vt, km, vtm):
    b, seq, _ = q.shape
    tq, ng = TQ_DIFF, NG_DIFF
    rows = [jnp.pad(p, (0, PAIR - HEAD_DIM)) for p in lam_params] + [g]
    par = jnp.pad(jnp.stack(rows), ((0, PARAM_ROWS - len(rows)), (0, 0)))
    in_specs, out_spec = _attn_specs(b, seq, tq, ng, kv_buffers=2)
    return pl.pallas_call(
        _diff_kernel,
        grid=(b, N_PAIR // ng, seq // tq),
        in_specs=[pl.BlockSpec((PARAM_ROWS, PAIR), lambda bi, hg, i: (0, 0))] + in_specs,
        out_specs=out_spec,
        out_shape=jax.ShapeDtypeStruct((b, seq, D_MODEL), BF16),
        scratch_shapes=[pltpu.VMEM((ng, 2 * tq, PAIR), BF16),
                        pltpu.VMEM((2, ng, TK, 2 * tq), BF16),
                        pltpu.VMEM((ng, PAIR + SUM_ROWS, 2 * tq), F32),
                        pltpu.VMEM((ng, 1, 2 * tq), F32)],
        compiler_params=pltpu.CompilerParams(
            dimension_semantics=("arbitrary", "arbitrary", "arbitrary"), vmem_limit_bytes=VMEM_LIMIT),
        name="diff_attn",
    )(par, q, k, vt, km, vtm)


def _sb_kernel(q_ref, k_ref, vt_ref, km_ref, vtm_ref, o_ref, qq_ref, z_ref, acc_ref, c_ref):
    i = pl.program_id(2)
    tq = q_ref.shape[1]
    tk = vt_ref.shape[4]
    ng = qq_ref.shape[0]

    ur = lax.broadcasted_iota(jnp.int32, (tk, tk), 0)
    uc = lax.broadcasted_iota(jnp.int32, (tk, tk), 1)
    u = jnp.where(uc >= ur, 1.0, 0.0).astype(BF16)

    for h in range(ng):
        qq_ref[h] = _split_pair(q_ref[0, :, _lanes(h)])
    acc_ref[...] = jnp.zeros_like(acc_ref)
    c_ref[...] = jnp.zeros_like(c_ref)

    def stage_logits(j, slot, groups):
        rows = pl.ds(pl.multiple_of(j * tk, tk), tk)
        for h in groups:
            z_ref[slot, h] = _nt_dot(k_ref[0, rows, _lanes(h)], qq_ref[h])

    def staged(slot):
        return lambda h: z_ref[slot, h]

    def staging(j, slot):
        done = set()

        def load(h):
            for g in (h, h + 1):
                if g < ng and g not in done:
                    stage_logits(j, slot, [g])
                    done.add(g)
            return z_ref[slot, h]
        return load

    half = tk // 2
    tri = (lax.broadcasted_iota(jnp.int32, (half, half), 0)
           < lax.broadcasted_iota(jnp.int32, (half, half), 1))

    def on_diagonal(fn, *arrays):
        rows_out = []
        for rb in range(2):
            tiles = []
            for cb in range(2 * tq // half):
                qh = cb % 2
                if rb > qh:
                    tiles.append(jnp.zeros((half, half), F32))
                    continue
                t = fn(*(a[rb * half:(rb + 1) * half, cb * half:(cb + 1) * half] for a in arrays))
                tiles.append(jnp.where(tri, t, 0.0) if rb == qh else t)
            rows_out.append(jnp.concatenate(tiles, axis=1))
        return jnp.concatenate(rows_out, axis=0)

    def diagonal_sums(hi, lo):
        def cols(x, qh, rows):
            return jnp.concatenate([x[rows, (2 * g + qh) * half:(2 * g + qh + 1) * half] for g in range(2)], axis=1)

        first_rows, all_rows = slice(0, half), slice(0, tk)
        early = (_dot(u[:half, :half], cols(hi, 0, first_rows))
                 + _dot(u[:half, :half], cols(lo, 0, first_rows)))
        late = _dot(u, cols(hi, 1, all_rows)) + _dot(u, cols(lo, 1, all_rows))
        unseen = jnp.zeros((half, half), F32)
        top = [early[:, :half], late[:half, :half], early[:, half:], late[:half, half:]]
        bottom = [unseen, late[half:, :half], unseen, late[half:, half:]]
        return jnp.concatenate([jnp.concatenate(top, axis=1), jnp.concatenate(bottom, axis=1)], axis=0)

    def softplus2(z):
        return jnp.maximum(z, 0.0) + jnp.log2(1.0 + jnp.exp2(-jnp.abs(z)))

    def update(load_z, vts, u_blk, visible):
        parts = []
        for h in range(ng):
            z = load_z(h)
            if visible is DIAGONAL:
                sp = on_diagonal(softplus2, z)
            else:
                sp = softplus2(z)
                if visible is not None:
                    sp = jnp.where(visible, sp, 0.0)
            hi = sp.astype(BF16)
            parts.append((hi, (sp - hi.astype(F32)).astype(BF16)))
        if visible is DIAGONAL:
            tots = [diagonal_sums(hi, lo) + c_ref[h] for h, (hi, lo) in enumerate(parts)]
        else:
            tots = [_dot(u_blk, hi) + _dot(u_blk, lo) + c_ref[h] for h, (hi, lo) in enumerate(parts)]
        weights = []
        for h, tot in enumerate(tots):
            if visible is DIAGONAL:
                a = on_diagonal(lambda z, t: jnp.exp2(z - t), load_z(h), tot)
            else:
                a = jnp.exp2(load_z(h) - tot)
                if visible is not None:
                    a = jnp.where(visible, a, 0.0)
            weights.append(a.astype(BF16))
            c_ref[h] = tot[0:1, :]
        for h, (a, vt) in enumerate(zip(weights, vts)):
            acc_ref[h, 0:HEAD_DIM, :] += _dot(vt[0:HEAD_DIM, :], a[:, :tq])
            acc_ref[h, HEAD_DIM:PAIR, :] += _dot(vt[HEAD_DIM:PAIR, :], a[:, tq:])

    def min_carry():
        cm = c_ref[0]
        for h in range(1, ng):
            cm = jnp.minimum(cm, c_ref[h])
        return jnp.min(cm)

    def sweep(j, load_z, visible):
        update(load_z, [vt_ref[0, h, j] for h in range(ng)], u, visible)

    stage_logits(i, 0, range(ng))
    sweep(i, staged(0), DIAGONAL)

    @pl.when(jnp.logical_and(i >= 1, min_carry() < SB_DONE_BITS))
    def _():
        sweep(i - 1, staging(i - 1, 1), None)

    def cond(state):
        t, cmin = state
        return jnp.logical_and(t <= i, cmin < SB_DONE_BITS)

    def body(state):
        t, _ = state
        sweep(i - t, staging(i - t, 0), None)
        return t + 1, min_carry()

    _, cmin = lax.while_loop(cond, body, (jnp.int32(2), min_carry()))

    @pl.when(cmin < SB_DONE_BITS)
    def _():
        rowm = lax.broadcasted_iota(jnp.int32, (META_PAD, 2 * tq), 0)
        zm = [_nt_dot(km_ref[:, _lanes(h)], qq_ref[h]) for h in range(ng)]
        update(lambda h: zm[h], [vtm_ref[h] for h in range(ng)], u[:META_PAD, :META_PAD], rowm < N_META)

    for h in range(ng):
        o_ref[0, :, _lanes(h)] = acc_ref[h].T.astype(o_ref.dtype)


def _sb_attn(q, k, vt, km, vtm):
    b, seq, _ = q.shape
    tq, ng = TQ_SB, NG_SB
    in_specs, out_spec = _attn_specs(b, seq, tq, ng, kv_buffers=1)
    return pl.pallas_call(
        _sb_kernel,
        grid=(b, N_PAIR // ng, seq // tq),
        in_specs=in_specs,
        out_specs=out_spec,
        out_shape=jax.ShapeDtypeStruct((b, seq, D_MODEL), BF16),
        scratch_shapes=[pltpu.VMEM((ng, 2 * tq, PAIR), BF16),
                        pltpu.VMEM((2, ng, TK, 2 * tq), F32),
                        pltpu.VMEM((ng, PAIR, tq), F32),
                        pltpu.VMEM((ng, 1, 2 * tq), F32)],
        compiler_params=pltpu.CompilerParams(
            dimension_semantics=("arbitrary", "arbitrary", "arbitrary"), vmem_limit_bytes=VMEM_LIMIT),
        name="sb_attn",
    )(q, k, vt, km, vtm)


def _tail_kernel(x_ref, oa_ref, ob_ref, ga_ref, gb_ref, wa_ref, wb_ref, wo_ref, gm_ref,
                 wu_ref, wd_ref, gf_ref, o_ref, *, tf):
    ua = _dot(oa_ref[...], wa_ref[...])
    ub = _dot(ob_ref[...], wb_ref[...])
    merged = jax.nn.sigmoid(ga_ref[...]) * ua + jax.nn.sigmoid(gb_ref[...]) * ub
    h = x_ref[...] + _dot(merged.astype(BF16), wo_ref[...])
    r = lax.rsqrt(jnp.mean(h * h, axis=-1, keepdims=True) + EPS)
    hm = (h * r * gm_ref[...]).astype(BF16)
    for c in range(D_FF // tf):
        up = jnp.maximum(_dot(hm, wu_ref[:, c * tf:(c + 1) * tf]), 0.0)
        h = h + _dot((up * up).astype(BF16), wd_ref[c * tf:(c + 1) * tf, :])
    r = lax.rsqrt(jnp.mean(h * h, axis=-1, keepdims=True) + EPS)
    o_ref[...] = h * r * gf_ref[...]


def _tail(x2d, oa, ob, ga, gb, wa, wb, wo, gm, wu, wd, gf, *, tm, tf):
    m = x2d.shape[0]
    row = pl.BlockSpec((tm, D_MODEL), lambda i: (i, 0))
    vec = pl.BlockSpec((1, D_MODEL), lambda i: (0, 0))

    def resident(shape):
        return pl.BlockSpec(shape, lambda i: (0, 0), pipeline_mode=pl.Buffered(1))

    return pl.pallas_call(
        functools.partial(_tail_kernel, tf=tf),
        grid=(m // tm,),
        in_specs=[row, row, row, row, row,
                  resident((D_MODEL, D_MODEL)), resident((D_MODEL, D_MODEL)), resident((D_MODEL, D_MODEL)), vec,
                  resident((D_MODEL, D_FF)), resident((D_FF, D_MODEL)), vec],
        out_specs=row,
        out_shape=jax.ShapeDtypeStruct((m, D_MODEL), F32),
        compiler_params=pltpu.CompilerParams(
            dimension_semantics=("arbitrary",), vmem_limit_bytes=VMEM_LIMIT_WIDE),
        name="tail",
    )(x2d, oa, ob, ga, gb, wa, wb, wo, gm, wu, wd, gf)


def kernel(x, meta_tokens, norm_mix_g, w_in, lambda_q1, lambda_k1, lambda_q2, lambda_k2, subln_g,
           w_out_a, w_out_b, w_out, norm_mlp_g, w_up, w_down, norm_final_g):
    b, seq, d = x.shape
    assert d == D_MODEL and seq % TQ_DIFF == 0 and TQ_DIFF % TK == 0 and TQ_SB == TK
    assert w_in.shape == (1, D_MODEL, N_SEG * D_MODEL)
    x2d = x.reshape(b * seq, d)
    g_mix = norm_mix_g[0].reshape(1, d)
    w_in_b = w_in[0].astype(BF16)

    cos_f, sin_f = _rope_tables(jnp.arange(N_META, N_META + seq, dtype=jnp.int32))
    qa, ka, vat, qb, kb, vbt, ga, gb = _inproj(
        x2d, g_mix, w_in_b, cos_f, sin_f, rows_per_batch=seq, tm=512, tk=TK)

    meta_pad = jnp.zeros((META_PAD, d), x.dtype).at[:N_META].set(meta_tokens.astype(x.dtype))
    cos_m, sin_m = _rope_tables(jnp.arange(META_PAD, dtype=jnp.int32))
    _, kam, vatm, _, kbm, vbtm, _, _ = _inproj(
        meta_pad, g_mix, w_in_b, cos_m, sin_m, rows_per_batch=META_PAD, tm=META_PAD, tk=META_PAD)
    vatm = vatm.reshape(N_PAIR, PAIR, META_PAD)
    vbtm = vbtm.reshape(N_PAIR, PAIR, META_PAD)

    lam_params = [p[0].astype(F32) for p in (lambda_q1, lambda_k1, lambda_q2, lambda_k2)]
    oa = _diff_attn(lam_params, subln_g[0].astype(F32),
                    qa.reshape(b, seq, d), ka.reshape(b, seq, d), vat, kam, vatm)
    ob = _sb_attn(qb.reshape(b, seq, d), kb.reshape(b, seq, d), vbt, kbm, vbtm)

    out = _tail(x2d, oa.reshape(b * seq, d), ob.reshape(b * seq, d), ga, gb,
                w_out_a[0].astype(BF16), w_out_b[0].astype(BF16), w_out[0].astype(BF16),
                norm_mlp_g[0].reshape(1, d), w_up[0].astype(BF16), w_down[0].astype(BF16),
                norm_final_g.reshape(1, d), tm=512, tf=512)
    return out.reshape(b, seq, d)
```
